```python
import math
import jax, jax.numpy as jnp
from jax import lax
import numpy as np

D_MODEL = 1024
BATCH = 8
SEQ = 4096
DEPTH = 4

N_META = 16
BLOCK_Q = 128
N_A_LAYERS = DEPTH // 2
N_B_LAYERS = DEPTH - N_A_LAYERS
SB_HEADS = 16
SB_HEAD_DIM = D_MODEL // SB_HEADS
DIFF_HEADS = 8
DIFF_QK_DIM = D_MODEL // DIFF_HEADS // 2
DIFF_V_DIM = 2 * DIFF_QK_DIM
ROPE_DIM = DIFF_QK_DIM // 4
ROPE_THETA = 500000.0
D_FF = 4 * D_MODEL
EPS = 1e-6

kernel_name = "yoco_stickbreak_diffattn_hybrid"


def rmsnorm(x, g):
    xf = x.astype(jnp.float32)
    y = xf * lax.rsqrt(jnp.mean(xf * xf, axis=-1, keepdims=True) + EPS)
    return (y * g.astype(jnp.float32)).astype(x.dtype)


def rope_tables(T):
    inv_freq = jnp.power(ROPE_THETA, -jnp.arange(0, ROPE_DIM, 2, dtype=jnp.float32) / ROPE_DIM)
    ang = jnp.arange(T, dtype=jnp.float32)[:, None] * inv_freq[None, :]
    return jnp.cos(ang), jnp.sin(ang)


def apply_partial_rope(x, cos, sin):
    half = ROPE_DIM // 2
    x1 = x[..., :half]
    x2 = x[..., half:ROPE_DIM]
    c = cos[:, None, :]
    s = sin[:, None, :]
    return jnp.concatenate([x1 * c - x2 * s, x2 * c + x1 * s, x[..., ROPE_DIM:]], axis=-1)


def to_blocks(q):
    B, H, T, D = q.shape
    return q.reshape(B, H, T // BLOCK_Q, BLOCK_Q, D).transpose(2, 0, 1, 3, 4)


def from_blocks(o):
    NB, B, H, BQ, D = o.shape
    return o.transpose(1, 2, 0, 3, 4).reshape(B, H, NB * BQ, D)


def merge_heads(o):
    B, H, T, D = o.shape
    return o.transpose(0, 2, 1, 3).reshape(B, T, H * D)


def stick_breaking_attention(q, k, v):
    T = q.shape[2]
    scale = SB_HEAD_DIM ** -0.5
    kf = k.astype(jnp.float32)
    key_pos = jnp.arange(T)

    def one_block(args):
        qb, q0 = args
        z = jnp.einsum('bhqd,bhkd->bhqk', qb.astype(jnp.float32), kf) * scale
        q_pos = q0 + jnp.arange(BLOCK_Q)
        past = key_pos[None, :] < q_pos[:, None]
        log_keep = jnp.where(past, jax.nn.log_sigmoid(-z), 0.0)
        after = lax.cumsum(log_keep, axis=3, reverse=True) - log_keep
        w = jnp.where(past, jnp.exp(jax.nn.log_sigmoid(z) + after), 0.0)
        return jnp.einsum('bhqk,bhkd->bhqd', w.astype(v.dtype), v)

    starts = jnp.arange(T // BLOCK_Q) * BLOCK_Q
    return from_blocks(lax.map(one_block, (to_blocks(q), starts)))


def differential_attention(q1, q2, k1, k2, v, lam):
    T = q1.shape[2]
    scale = DIFF_QK_DIM ** -0.5
    k1f = k1.astype(jnp.float32)
    k2f = k2.astype(jnp.float32)
    key_pos = jnp.arange(T)

    def one_block(args):
        q1b, q2b, q0 = args
        q_pos = q0 + jnp.arange(BLOCK_Q)
        causal = key_pos[None, :] <= q_pos[:, None]

        def probs(qb, kk):
            s = jnp.einsum('bhqd,bhkd->bhqk', qb.astype(jnp.float32), kk) * scale
            return jax.nn.softmax(jnp.where(causal, s, -jnp.inf), axis=-1)

        w = probs(q1b, k1f) - lam * probs(q2b, k2f)
        return jnp.einsum('bhqk,bhkd->bhqd', w.astype(v.dtype), v)

    starts = jnp.arange(T // BLOCK_Q) * BLOCK_Q
    return from_blocks(lax.map(one_block, (to_blocks(q1), to_blocks(q2), starts)))


def squared_relu_mlp(x, g, w_in, w_out):
    h = rmsnorm(x, g)
    return jnp.square(jax.nn.relu(h @ w_in)) @ w_out


def setup_inputs(seed: int = 0) -> dict:
    key = jax.random.key(seed)
    ks = jax.random.split(key, 20)
    nrm = jax.random.normal
    f32 = jnp.float32

    def w(k, shape, fan_in):
        return nrm(k, shape, f32) * fan_in ** -0.5

    def gain(k, shape):
        return 1.0 + 0.05 * nrm(k, shape, f32)

    return {
        "x": nrm(ks[0], (BATCH, SEQ, D_MODEL), f32),
        "meta_tokens": nrm(ks[1], (N_META, D_MODEL), f32),
        "a_norm_g": gain(ks[2], (N_A_LAYERS, D_MODEL)),
        "a_w_qkv": w(ks[3], (N_A_LAYERS, D_MODEL, 3 * D_MODEL), D_MODEL),
        "a_w_o": w(ks[4], (N_A_LAYERS, D_MODEL, D_MODEL), D_MODEL),
        "kv_norm_g": gain(ks[5], (D_MODEL,)),
        "kv_w": w(ks[6], (D_MODEL, DIFF_HEADS * (2 * DIFF_QK_DIM + DIFF_V_DIM)), D_MODEL),
        "kv_k_norm_g": gain(ks[7], (2, DIFF_QK_DIM)),
        "b_norm_g": gain(ks[8], (N_B_LAYERS, D_MODEL)),
        "b_w_q": w(ks[9], (N_B_LAYERS, D_MODEL, DIFF_HEADS * 2 * DIFF_QK_DIM), D_MODEL),
        "b_q_norm_g": gain(ks[10], (N_B_LAYERS, 2, DIFF_QK_DIM)),
        "b_lambda": 0.1 * nrm(ks[11], (N_B_LAYERS, 4, DIFF_QK_DIM), f32),
        "b_subln_g": gain(ks[12], (N_B_LAYERS, DIFF_V_DIM)),
        "b_w_o": w(ks[13], (N_B_LAYERS, DIFF_HEADS * DIFF_V_DIM, D_MODEL), DIFF_HEADS * DIFF_V_DIM),
        "mlp_norm_g": gain(ks[14], (DEPTH, D_MODEL)),
        "mlp_w_in": w(ks[15], (DEPTH, D_MODEL, D_FF), D_MODEL),
        "mlp_w_out": w(ks[16], (DEPTH, D_FF, D_MODEL), D_FF),
    }


def reference(x, meta_tokens, a_norm_g, a_w_qkv, a_w_o, kv_norm_g, kv_w, kv_k_norm_g,
              b_norm_g, b_w_q, b_q_norm_g, b_lambda, b_subln_g, b_w_o,
              mlp_norm_g, mlp_w_in, mlp_w_out):
    B, S, D = x.shape
    L = S + N_META
    pad = (-L) % BLOCK_Q
    meta = jnp.broadcast_to(meta_tokens.astype(x.dtype)[None], (B, N_META, D))
    h = jnp.concatenate([meta, x, jnp.zeros((B, pad, D), x.dtype)], axis=1)
    T = L + pad
    cos, sin = rope_tables(T)

    shared = None
    for layer in range(DEPTH):
        if layer < N_A_LAYERS:
            i = layer
            hn = rmsnorm(h, a_norm_g[i])
            qkv = (hn @ a_w_qkv[i]).reshape(B, T, 3, SB_HEADS, SB_HEAD_DIM)
            q = qkv[:, :, 0].transpose(0, 2, 1, 3)
            k = qkv[:, :, 1].transpose(0, 2, 1, 3)
            v = qkv[:, :, 2].transpose(0, 2, 1, 3)
            o = stick_breaking_attention(q, k, v)
            h = h + merge_heads(o) @ a_w_o[i]
        else:
            j = layer - N_A_LAYERS
            if shared is None:
                kvn = rmsnorm(h, kv_norm_g)
                kv = kvn @ kv_w
                kk = kv[..., :DIFF_HEADS * 2 * DIFF_QK_DIM].reshape(B, T, DIFF_HEADS, 2, DIFF_QK_DIM)
                k1 = apply_partial_rope(rmsnorm(kk[..., 0, :], kv_k_norm_g[0]), cos, sin)
                k2 = apply_partial_rope(rmsnorm(kk[..., 1, :], kv_k_norm_g[1]), cos, sin)
                vv = kv[..., DIFF_HEADS * 2 * DIFF_QK_DIM:].reshape(B, T, DIFF_HEADS, DIFF_V_DIM)
                shared = (k1.transpose(0, 2, 1, 3), k2.transpose(0, 2, 1, 3), vv.transpose(0, 2, 1, 3))
            k1, k2, vv = shared
            lam_init = 0.8 - 0.6 * math.exp(-0.3 * layer)
            lp = b_lambda[j].astype(jnp.float32)
            lam = jnp.exp(jnp.sum(lp[0] * lp[1])) - jnp.exp(jnp.sum(lp[2] * lp[3])) + lam_init
            hn = rmsnorm(h, b_norm_g[j])
            qq = (hn @ b_w_q[j]).reshape(B, T, DIFF_HEADS, 2, DIFF_QK_DIM)
            q1 = apply_partial_rope(rmsnorm(qq[..., 0, :], b_q_norm_g[j, 0]), cos, sin)
            q2 = apply_partial_rope(rmsnorm(qq[..., 1, :], b_q_norm_g[j, 1]), cos, sin)
            o = differential_attention(q1.transpose(0, 2, 1, 3), q2.transpose(0, 2, 1, 3), k1, k2, vv, lam)
            o = rmsnorm(o, b_subln_g[j]) * (1.0 - lam_init)
            h = h + merge_heads(o).astype(h.dtype) @ b_w_o[j]
        h = h + squared_relu_mlp(h, mlp_norm_g[layer], mlp_w_in[layer], mlp_w_out[layer])

    return h[:, N_META:N_META + S]
```

```python
import functools
import math

import jax
import jax.numpy as jnp
from jax import lax
from jax.experimental import pallas as pl
from jax.experimental.pallas import tpu as pltpu

N_META = 16
SB_HEADS = 16
HEAD_DIM = 64
DIFF_HEADS = 8
ROPE_DIM = 16
ROPE_THETA = 500000.0
EPS = 1e-6

LANES = 128
ATTN_BLOCK = 256
ROW_TILE = 512
FF_CHUNK = 1024
VMEM_LIMIT = 56 * 1024 * 1024

F32 = jnp.float32
BF16 = jnp.bfloat16
_NT = (((1,), (1,)), ((), ()))


def _dot(a, b):
    return jnp.dot(a, b, preferred_element_type=F32)


def _rms_scale(x):
    return lax.rsqrt(jnp.mean(x * x, axis=-1, keepdims=True) + EPS)


def _resident(shape):
    nd = len(shape)
    return pl.BlockSpec(shape, lambda *_: (0,) * nd, pipeline_mode=pl.Buffered(1))


def _norm_proj_kernel(h_ref, g_ref, w_ref, *rest, n_rope_cols):
    if n_rope_cols:
        sg_ref, cos_ref, s1_ref, s2_ref, o_ref = rest
    else:
        (o_ref,) = rest
    x = h_ref[...]
    hn = (x * _rms_scale(x) * g_ref[...]).astype(BF16)
    n_out = o_ref.shape[1]
    if n_rope_cols:
        r = lax.broadcasted_iota(jnp.int32, (LANES, LANES), 0) // HEAD_DIM
        c = lax.broadcasted_iota(jnp.int32, (LANES, LANES), 1) // HEAD_DIM
        seg = jnp.where(r == c, 1.0, 0.0).astype(BF16)
        cos, s1, s2 = cos_ref[...], s1_ref[...], s2_ref[...]
    chunk = min(FF_CHUNK, n_out)
    for c0 in range(0, n_out, chunk):
        y = _dot(hn, w_ref[:, c0:c0 + chunk])
        if c0 >= n_rope_cols:
            o_ref[:, c0:c0 + chunk] = y.astype(o_ref.dtype)
            continue
        for l0 in range(0, chunk, LANES):
            yc = y[:, l0:l0 + LANES]
            sq = yc * yc
            hi = sq.astype(BF16)
            lo = (sq - hi.astype(F32)).astype(BF16)
            ms = (_dot(hi, seg) + _dot(lo, seg)) * (1.0 / HEAD_DIM)
            yn = yc * lax.rsqrt(ms + EPS) * sg_ref[:, c0 + l0:c0 + l0 + LANES]
            rot = (yn * cos
                   + pltpu.roll(yn, LANES - ROPE_DIM // 2, 1) * s1
                   + pltpu.roll(yn, ROPE_DIM // 2, 1) * s2)
            o_ref[:, c0 + l0:c0 + l0 + LANES] = rot.astype(o_ref.dtype)


def _norm_proj(h, g, w, *, tm, seg_gain=None, rope=None, n_rope_cols=0, seq_len=None):
    n, d = h.shape
    n_out = w.shape[1]
    in_specs = [pl.BlockSpec((tm, d), lambda i: (i, 0)), _resident((1, d)), _resident((d, n_out))]
    args = [h, g.reshape(1, d), w]
    if n_rope_cols:
        tiles_per_seq = seq_len // tm
        tab = pl.BlockSpec((tm, LANES), lambda i: (i % tiles_per_seq, 0))
        in_specs += [_resident((1, n_rope_cols)), tab, tab, tab]
        args += [seg_gain.reshape(1, n_rope_cols), *rope]
    return pl.pallas_call(
        functools.partial(_norm_proj_kernel, n_rope_cols=n_rope_cols),
        grid=(n // tm,),
        in_specs=in_specs,
        out_specs=pl.BlockSpec((tm, n_out), lambda i: (i, 0)),
        out_shape=jax.ShapeDtypeStruct((n, n_out), BF16),
        compiler_params=pltpu.CompilerParams(
            dimension_semantics=("parallel",), vmem_limit_bytes=VMEM_LIMIT),
        name="norm_proj_rope" if n_rope_cols else "norm_proj",
    )(*args)


def _head_pair(q):
    lane = lax.broadcasted_iota(jnp.int32, (1, LANES), 1)
    zero = jnp.zeros_like(q)
    return jnp.where(lane < HEAD_DIM, q, zero), jnp.where(lane >= HEAD_DIM, q, zero)


def _sb_attn_kernel(q_ref, k_ref, v_ref, o_ref):
    qi = pl.program_id(2)
    blk = q_ref.shape[1]
    qh = _head_pair(q_ref[0])
    row = lax.broadcasted_iota(jnp.int32, (blk, blk), 0)
    col = lax.broadcasted_iota(jnp.int32, (blk, blk), 1)
    tri = jnp.where(row >= col, 1.0, 0.0).astype(BF16)
    past = col < row

    def tile(j, state, masked):
        start = pl.multiple_of(j * blk, blk)
        kb = k_ref[0, pl.ds(start, blk), :]
        vb = v_ref[0, pl.ds(start, blk), :]
        new_state = []
        for hh in range(2):
            carry, acc = state[hh]
            z = lax.dot_general(qh[hh], kb, _NT, preferred_element_type=F32)
            lk = -(jnp.maximum(z, 0.0) + jnp.log(1.0 + jnp.exp(-jnp.abs(z))))
            if masked:
                lk = jnp.where(past, lk, 0.0)
            hi = lk.astype(BF16)
            lo = (lk - hi.astype(F32)).astype(BF16)
            cum = _dot(hi, tri) + _dot(lo, tri)
            p = jnp.exp(z + cum + carry)
            if masked:
                p = jnp.where(past, p, 0.0)
            acc = acc + _dot(p.astype(BF16), vb)
            carry = carry + cum[:, 0:1]
            new_state.append((carry, acc))
        return tuple(new_state)

    init = tuple((jnp.zeros((blk, 1), F32), jnp.zeros((blk, LANES), F32)) for _ in range(2))
    state = tile(qi, init, True)
    state = lax.fori_loop(0, qi, lambda i, st: tile(qi - 1 - i, st, False), state)
    lane = lax.broadcasted_iota(jnp.int32, (1, LANES), 1)
    o_ref[0] = jnp.where(lane < HEAD_DIM, state[0][1], state[1][1]).astype(o_ref.dtype)


def _sb_attention(qkv, batch, seq_len):
    d = SB_HEADS * HEAD_DIM
    pairs = d // LANES
    qkv3 = qkv.reshape(batch, seq_len, 3 * d)
    out = pl.pallas_call(
        _sb_attn_kernel,
        grid=(batch, pairs, seq_len // ATTN_BLOCK),
        in_specs=[
            pl.BlockSpec((1, ATTN_BLOCK, LANES), lambda b, p, i: (b, i, p)),
            pl.BlockSpec((1, seq_len, LANES), lambda b, p, i: (b, 0, pairs + p)),
            pl.BlockSpec((1, seq_len, LANES), lambda b, p, i: (b, 0, 2 * pairs + p)),
        ],
        out_specs=pl.BlockSpec((1, ATTN_BLOCK, LANES), lambda b, p, i: (b, i, p)),
        out_shape=jax.ShapeDtypeStruct((batch, seq_len, d), BF16),
        compiler_params=pltpu.CompilerParams(
            dimension_semantics=("parallel", "parallel", "arbitrary"), vmem_limit_bytes=VMEM_LIMIT),
        name="sb_attention",
    )(qkv3, qkv3, qkv3)
    return out.reshape(batch * seq_len, d)


def _diff_attn_kernel(lam_ref, q_ref, k_ref, v_ref, g_ref, o_ref, *, lam_init):
    qi = pl.program_id(2)
    blk = q_ref.shape[1]
    qh = _head_pair(q_ref[0])
    row = lax.broadcasted_iota(jnp.int32, (blk, blk), 0)
    col = lax.broadcasted_iota(jnp.int32, (blk, blk), 1)
    causal = col <= row

    def tile(j, state, masked):
        start = pl.multiple_of(j * blk, blk)
        kb = k_ref[0, pl.ds(start, blk), :]
        vb = v_ref[0, pl.ds(start, blk), :]
        new_state = []
        for hh in range(2):
            s = lax.dot_general(qh[hh], kb, _NT, preferred_element_type=F32)
            if masked:
                s = jnp.where(causal, s, -jnp.inf)
            s_max = jnp.max(s, axis=-1, keepdims=True)
            if state is None:
                m_new = s_max
                p = jnp.exp(s - m_new)
                l_new = jnp.sum(p, axis=-1, keepdims=True)
                acc = _dot(p.astype(BF16), vb)
            else:
                m, l, acc = state[hh]
                m_new = jnp.maximum(m, s_max)
                alpha = jnp.exp(m - m_new)
                p = jnp.exp(s - m_new)
                l_new = alpha * l + jnp.sum(p, axis=-1, keepdims=True)
                acc = alpha * acc + _dot(p.astype(BF16), vb)
            new_state.append((m_new, l_new, acc))
        return tuple(new_state)

    state = tile(qi, None, True)
    state = lax.fori_loop(0, qi, lambda j, st: tile(j, st, False), state)

    lp = lam_ref[...]
    lam = (jnp.exp(jnp.sum(lp[0:1] * lp[1:2], axis=-1, keepdims=True))
           - jnp.exp(jnp.sum(lp[2:3] * lp[3:4], axis=-1, keepdims=True)) + lam_init)
    (_, l1, a1), (_, l2, a2) = state
    o = a1 / l1 - lam * (a2 / l2)
    o = o * _rms_scale(o) * g_ref[...] * (1.0 - lam_init)
    o_ref[0] = o.astype(o_ref.dtype)


def _diff_attention(q, kv, lam_params, subln_g, lam_init, batch, seq_len):
    d = DIFF_HEADS * LANES
    q3 = q.reshape(batch, seq_len, d)
    kv3 = kv.reshape(batch, seq_len, 2 * d)
    out = pl.pallas_call(
        functools.partial(_diff_attn_kernel, lam_init=lam_init),
        grid=(batch, DIFF_HEADS, seq_len // ATTN_BLOCK),
        in_specs=[
            pl.BlockSpec(lam_params.shape, lambda b, h, i: (0, 0)),
            pl.BlockSpec((1, ATTN_BLOCK, LANES), lambda b, h, i: (b, i, h)),
            pl.BlockSpec((1, seq_len, LANES), lambda b, h, i: (b, 0, h)),
            pl.BlockSpec((1, seq_len, LANES), lambda b, h, i: (b, 0, DIFF_HEADS + h)),
            pl.BlockSpec((1, LANES), lambda b, h, i: (0, 0)),
        ],
        out_specs=pl.BlockSpec((1, ATTN_BLOCK, LANES), lambda b, h, i: (b, i, h)),
        out_shape=jax.ShapeDtypeStruct((batch, seq_len, d), BF16),
        compiler_params=pltpu.CompilerParams(
            dimension_semantics=("parallel", "parallel", "arbitrary"), vmem_limit_bytes=VMEM_LIMIT),
        name="diff_attention",
    )(lam_params, q3, kv3, kv3, subln_g.reshape(1, LANES))
    return out.reshape(batch * seq_len, d)


def _post_attn_kernel(h_ref, o_ref, wo_ref, g_ref, win_ref, wout_ref, out_ref):
    h1 = h_ref[...] + _dot(o_ref[...], wo_ref[...])
    hn = (h1 * _rms_scale(h1) * g_ref[...]).astype(BF16)
    acc = h1
    d_ff = win_ref.shape[1]
    for c0 in range(0, d_ff, FF_CHUNK):
        a = jnp.maximum(_dot(hn, win_ref[:, c0:c0 + FF_CHUNK]), 0.0)
        acc = acc + _dot((a * a).astype(BF16), wout_ref[c0:c0 + FF_CHUNK, :])
    out_ref[...] = acc


def _post_attention(h, o, wo, g, w_in, w_out, *, tm):
    n, d = h.shape
    row = pl.BlockSpec((tm, d), lambda i: (i, 0))
    return pl.pallas_call(
        _post_attn_kernel,
        grid=(n // tm,),
        in_specs=[row, pl.BlockSpec((tm, o.shape[1]), lambda i: (i, 0)), _resident(wo.shape),
                  _resident((1, d)), _resident(w_in.shape), _resident(w_out.shape)],
        out_specs=row,
        out_shape=jax.ShapeDtypeStruct((n, d), F32),
        compiler_params=pltpu.CompilerParams(
            dimension_semantics=("parallel",), vmem_limit_bytes=VMEM_LIMIT),
        name="post_attention",
    )(h, o, wo, g.reshape(1, d), w_in, w_out)


def _rope_tables(seq_len):
    half = ROPE_DIM // 2
    inv_freq = jnp.power(ROPE_THETA, -jnp.arange(0, ROPE_DIM, 2, dtype=F32) / ROPE_DIM)
    ang = jnp.arange(seq_len, dtype=F32)[:, None] * inv_freq[None, :]
    cos, sin = jnp.cos(ang), jnp.sin(ang)
    seg_lane = jnp.arange(LANES) % HEAD_DIM
    idx = seg_lane % half
    cos_t = jnp.where(seg_lane < ROPE_DIM, cos[:, idx], 1.0)
    s1_t = jnp.where(seg_lane < half, -sin[:, idx], 0.0)
    s2_t = jnp.where((seg_lane >= half) & (seg_lane < ROPE_DIM), sin[:, idx], 0.0)
    return cos_t, s1_t, s2_t


def kernel(x, meta_tokens, a_norm_g, a_w_qkv, a_w_o, kv_norm_g, kv_w, kv_k_norm_g, b_norm_g, b_w_q,
           b_q_norm_g, b_lambda, b_subln_g, b_w_o, mlp_norm_g, mlp_w_in, mlp_w_out):
    batch, s_len, d = x.shape
    n_a = a_w_qkv.shape[0]
    n_b = b_w_q.shape[0]
    real = s_len + N_META
    seq_len = -(-real // ATTN_BLOCK) * ATTN_BLOCK
    n_rows = batch * seq_len
    tm = ROW_TILE if n_rows % ROW_TILE == 0 else ATTN_BLOCK

    meta = jnp.broadcast_to(meta_tokens.astype(x.dtype)[None], (batch, N_META, d))
    h = jnp.concatenate([meta, x, jnp.zeros((batch, seq_len - real, d), x.dtype)], axis=1)
    h = h.reshape(n_rows, d)
    rope = _rope_tables(seq_len)
    k_cols = DIFF_HEADS * 2 * HEAD_DIM
    qk_scale = HEAD_DIM ** -0.5

    kv = None
    for layer in range(n_a + n_b):
        if layer < n_a:
            i = layer
            col_scale = jnp.concatenate([jnp.full((d,), qk_scale, F32), jnp.ones((2 * d,), F32)])
            w_qkv = (a_w_qkv[i] * col_scale).astype(BF16)
            qkv = _norm_proj(h, a_norm_g[i], w_qkv, tm=tm)
            o = _sb_attention(qkv, batch, seq_len)
            w_o = a_w_o[i]
        else:
            j = layer - n_a
            if kv is None:
                k_gain = jnp.tile(kv_k_norm_g.reshape(2 * HEAD_DIM), DIFF_HEADS)
                kv = _norm_proj(h, kv_norm_g, kv_w.astype(BF16), tm=ATTN_BLOCK, seg_gain=k_gain,
                                rope=rope, n_rope_cols=k_cols, seq_len=seq_len)
            lam_init = 0.8 - 0.6 * math.exp(-0.3 * layer)
            q_gain = jnp.tile(b_q_norm_g[j].reshape(2 * HEAD_DIM), DIFF_HEADS) * qk_scale
            q = _norm_proj(h, b_norm_g[j], b_w_q[j].astype(BF16), tm=ATTN_BLOCK, seg_gain=q_gain,
                           rope=rope, n_rope_cols=k_cols, seq_len=seq_len)
            o = _diff_attention(q, kv, b_lambda[j].astype(F32), b_subln_g[j], lam_init, batch, seq_len)
            w_o = b_w_o[j]
        h = _post_attention(h, o, w_o.astype(BF16), mlp_norm_g[layer], mlp_w_in[layer].astype(BF16),
                            mlp_w_out[layer].astype(BF16), tm=tm)

    return h.reshape(batch, seq_len, d)[:, N_META:N_META + s_len]
```

```python
import functools
import math

import jax
import jax.numpy as jnp
from jax import lax
from jax.experimental import pallas as pl
from jax.experimental.pallas import tpu as pltpu

N_META = 16
SB_HEADS = 16
HEAD_DIM = 64
DIFF_HEADS = 8
ROPE_DIM = 16
ROPE_THETA = 500000.0
EPS = 1e-6

LANES = 128
ATTN_BLOCK = 256
ROW_TILE = 512
FF_CHUNK = 1024
SB_PAIRS_PER_STEP = 2
DIFF_HEADS_PER_STEP = 2
VMEM_LIMIT = 56 * 1024 * 1024

F32 = jnp.float32
BF16 = jnp.bfloat16
_NT = (((1,), (1,)), ((), ()))


def _dot(a, b):
    return jnp.dot(a, b, preferred_element_type=F32)


def _rms_scale(x):
    return lax.rsqrt(jnp.mean(x * x, axis=-1, keepdims=True) + EPS)


def _resident(shape):
    nd = len(shape)
    return pl.BlockSpec(shape, lambda *_: (0,) * nd, pipeline_mode=pl.Buffered(1))


def _norm_proj_kernel(h_ref, g_ref, w_ref, *rest, n_rope_cols):
    if n_rope_cols:
        sg_ref, cos_ref, s1_ref, s2_ref, o_ref = rest
    else:
        (o_ref,) = rest
    x = h_ref[...]
    hn = (x * _rms_scale(x) * g_ref[...]).astype(BF16)
    n_out = o_ref.shape[1]
    if n_rope_cols:
        r = lax.broadcasted_iota(jnp.int32, (LANES, LANES), 0) // HEAD_DIM
        c = lax.broadcasted_iota(jnp.int32, (LANES, LANES), 1) // HEAD_DIM
        seg = jnp.where(r == c, 1.0, 0.0).astype(BF16)
        cos, s1, s2 = cos_ref[...], s1_ref[...], s2_ref[...]
    chunk = min(FF_CHUNK, n_out)
    for c0 in range(0, n_out, chunk):
        y = _dot(hn, w_ref[:, c0:c0 + chunk])
        if c0 >= n_rope_cols:
            o_ref[:, c0:c0 + chunk] = y.astype(o_ref.dtype)
            continue
        for l0 in range(0, chunk, LANES):
            yc = y[:, l0:l0 + LANES]
            sq = yc * yc
            hi = sq.astype(BF16)
            lo = (sq - hi.astype(F32)).astype(BF16)
            ms = (_dot(hi, seg) + _dot(lo, seg)) * (1.0 / HEAD_DIM)
            yn = yc * lax.rsqrt(ms + EPS) * sg_ref[:, c0 + l0:c0 + l0 + LANES]
            rot = (yn * cos
                   + pltpu.roll(yn, LANES - ROPE_DIM // 2, 1) * s1
                   + pltpu.roll(yn, ROPE_DIM // 2, 1) * s2)
            o_ref[:, c0 + l0:c0 + l0 + LANES] = rot.astype(o_ref.dtype)


def _norm_proj(h, g, w, *, tm, seg_gain=None, rope=None, n_rope_cols=0, seq_len=None):
    n, d = h.shape
    n_out = w.shape[1]
    in_specs = [pl.BlockSpec((tm, d), lambda i: (i, 0)), _resident((1, d)), _resident((d, n_out))]
    args = [h, g.reshape(1, d), w]
    if n_rope_cols:
        tiles_per_seq = seq_len // tm
        tab = pl.BlockSpec((tm, LANES), lambda i: (i % tiles_per_seq, 0))
        in_specs += [_resident((1, n_rope_cols)), tab, tab, tab]
        args += [seg_gain.reshape(1, n_rope_cols), *rope]
    return pl.pallas_call(
        functools.partial(_norm_proj_kernel, n_rope_cols=n_rope_cols),
        grid=(n // tm,),
        in_specs=in_specs,
        out_specs=pl.BlockSpec((tm, n_out), lambda i: (i, 0)),
        out_shape=jax.ShapeDtypeStruct((n, n_out), BF16),
        compiler_params=pltpu.CompilerParams(
            dimension_semantics=("parallel",), vmem_limit_bytes=VMEM_LIMIT),
        name="norm_proj_rope" if n_rope_cols else "norm_proj",
    )(*args)


def _stack_halves(q, lo_half):
    zero = jnp.zeros_like(q)
    return jnp.concatenate([jnp.where(lo_half, q, zero), jnp.where(lo_half, zero, q)], axis=0)


def _neg_abs(x):
    sign = jnp.uint32(0x80000000)
    return lax.bitcast_convert_type(lax.bitcast_convert_type(x, jnp.uint32) | sign, F32)


def _sb_attn_kernel(q_ref, k_ref, v_ref, o_ref, *, n_pairs):
    qi = pl.program_id(2)
    blk = q_ref.shape[1]
    pairs = range(n_pairs)
    lo_half = lax.broadcasted_iota(jnp.int32, (1, LANES), 1) < HEAD_DIM
    qs = [_stack_halves(q_ref[0, :, p * LANES:(p + 1) * LANES], lo_half) for p in pairs]
    row = lax.broadcasted_iota(jnp.int32, (2 * blk, blk), 0)
    col = lax.broadcasted_iota(jnp.int32, (2 * blk, blk), 1)
    neg_tri = jnp.where(row % blk >= col, -1.0, 0.0).astype(BF16)
    past = col < row % blk

    def tile(j, state, masked):
        carry, acc = state
        start = pl.multiple_of(j * blk, blk)
        kb = [k_ref[0, pl.ds(start, blk), p * LANES:(p + 1) * LANES] for p in pairs]
        vb = [v_ref[0, pl.ds(start, blk), p * LANES:(p + 1) * LANES] for p in pairs]
        y = [lax.dot_general(qs[p], kb[p], _NT, preferred_element_type=F32) for p in pairs]
        split = []
        for p in pairs:
            sp = jnp.maximum(y[p], 0.0) + jnp.log2(1.0 + jnp.exp2(_neg_abs(y[p])))
            if masked:
                sp = jnp.where(past, sp, 0.0)
            hi = sp.astype(BF16)
            lo = (sp - hi.astype(F32)).astype(BF16)
            split.append(jnp.concatenate([hi, lo], axis=1))
        cum = [_dot(split[p], neg_tri) for p in pairs]
        new_acc = []
        for p in pairs:
            w = jnp.exp2(y[p] + cum[p] + carry[p])
            if masked:
                w = jnp.where(past, w, 0.0)
            w = w.astype(BF16)
            zero = jnp.zeros_like(vb[p])
            v2 = jnp.concatenate([jnp.where(lo_half, vb[p], zero), jnp.where(lo_half, zero, vb[p])], axis=0)
            new_acc.append(acc[p] + _dot(jnp.concatenate([w[:blk], w[blk:]], axis=1), v2))
        return tuple(carry[p] + cum[p][:, 0:1] for p in pairs), tuple(new_acc)

    init = (tuple(jnp.zeros((2 * blk, 1), F32) for _ in pairs),
            tuple(jnp.zeros((blk, LANES), F32) for _ in pairs))
    state = tile(qi, init, True)
    _, acc = lax.fori_loop(0, qi, lambda i, st: tile(qi - 1 - i, st, False), state)
    for p in pairs:
        o_ref[0, :, p * LANES:(p + 1) * LANES] = acc[p].astype(o_ref.dtype)


def _sb_attention(qkv, batch, seq_len):
    d = SB_HEADS * HEAD_DIM
    width = SB_PAIRS_PER_STEP * LANES
    groups = d // width
    qkv3 = qkv.reshape(batch, seq_len, 3 * d)
    out = pl.pallas_call(
        functools.partial(_sb_attn_kernel, n_pairs=SB_PAIRS_PER_STEP),
        grid=(batch, groups, seq_len // ATTN_BLOCK),
        in_specs=[
            pl.BlockSpec((1, ATTN_BLOCK, width), lambda b, g, i: (b, i, g)),
            pl.BlockSpec((1, seq_len, width), lambda b, g, i: (b, 0, groups + g)),
            pl.BlockSpec((1, seq_len, width), lambda b, g, i: (b, 0, 2 * groups + g)),
        ],
        out_specs=pl.BlockSpec((1, ATTN_BLOCK, width), lambda b, g, i: (b, i, g)),
        out_shape=jax.ShapeDtypeStruct((batch, seq_len, d), BF16),
        compiler_params=pltpu.CompilerParams(
            dimension_semantics=("parallel", "parallel", "arbitrary"), vmem_limit_bytes=VMEM_LIMIT),
        name="sb_attention",
    )(qkv3, qkv3, qkv3)
    return out.reshape(batch * seq_len, d)


def _diff_attn_kernel(lam_ref, q_ref, k_ref, v_ref, g_ref, o_ref, *, n_heads, lam_init):
    qi = pl.program_id(2)
    blk = q_ref.shape[1]
    heads = range(n_heads)
    lo_half = lax.broadcasted_iota(jnp.int32, (1, LANES), 1) < HEAD_DIM
    qs = [_stack_halves(q_ref[0, :, h * LANES:(h + 1) * LANES], lo_half) for h in heads]
    row = lax.broadcasted_iota(jnp.int32, (2 * blk, blk), 0)
    col = lax.broadcasted_iota(jnp.int32, (2 * blk, blk), 1)
    causal = col <= row % blk

    def tile(j, state, masked):
        start = pl.multiple_of(j * blk, blk)
        kb = [k_ref[0, pl.ds(start, blk), h * LANES:(h + 1) * LANES] for h in heads]
        vb = [v_ref[0, pl.ds(start, blk), h * LANES:(h + 1) * LANES] for h in heads]
        s = [lax.dot_general(qs[h], kb[h], _NT, preferred_element_type=F32) for h in heads]
        m_new, l_new, p, alpha = [], [], [], []
        for h in heads:
            sh = jnp.where(causal, s[h], -jnp.inf) if masked else s[h]
            s_max = jnp.max(sh, axis=-1, keepdims=True)
            if state is None:
                m_new.append(s_max)
                ph = jnp.exp2(sh - s_max)
                l_new.append(jnp.sum(ph, axis=-1, keepdims=True))
            else:
                m, l, _ = state[h]
                m_new.append(jnp.maximum(m, s_max))
                alpha.append(jnp.exp2(m - m_new[h]))
                ph = jnp.exp2(sh - m_new[h])
                l_new.append(alpha[h] * l + jnp.sum(ph, axis=-1, keepdims=True))
            p.append(ph.astype(BF16))
        pv = [_dot(p[h], vb[h]) for h in heads]
        if state is None:
            return tuple((m_new[h], l_new[h], pv[h]) for h in heads)
        return tuple((m_new[h], l_new[h], alpha[h] * state[h][2] + pv[h]) for h in heads)

    state = tile(qi, None, True)
    state = lax.fori_loop(0, qi, lambda j, st: tile(j, st, False), state)

    lp = lam_ref[...]
    lam = (jnp.exp(jnp.sum(lp[0:1] * lp[1:2], axis=-1, keepdims=True))
           - jnp.exp(jnp.sum(lp[2:3] * lp[3:4], axis=-1, keepdims=True)) + lam_init)
    for h in heads:
        _, l, acc = state[h]
        o = acc / l
        o = o[:blk] - lam * o[blk:]
        o = o * _rms_scale(o) * g_ref[...] * (1.0 - lam_init)
        o_ref[0, :, h * LANES:(h + 1) * LANES] = o.astype(o_ref.dtype)


def _diff_attention(q, kv, lam_params, subln_g, lam_init, batch, seq_len):
    d = DIFF_HEADS * LANES
    width = DIFF_HEADS_PER_STEP * LANES
    groups = d // width
    q3 = q.reshape(batch, seq_len, d)
    kv3 = kv.reshape(batch, seq_len, 2 * d)
    out = pl.pallas_call(
        functools.partial(_diff_attn_kernel, n_heads=DIFF_HEADS_PER_STEP, lam_init=lam_init),
        grid=(batch, groups, seq_len // ATTN_BLOCK),
        in_specs=[
            pl.BlockSpec(lam_params.shape, lambda b, g, i: (0, 0)),
            pl.BlockSpec((1, ATTN_BLOCK, width), lambda b, g, i: (b, i, g)),
            pl.BlockSpec((1, seq_len, width), lambda b, g, i: (b, 0, g)),
            pl.BlockSpec((1, seq_len, width), lambda b, g, i: (b, 0, groups + g)),
            pl.BlockSpec((1, LANES), lambda b, g, i: (0, 0)),
        ],
        out_specs=pl.BlockSpec((1, ATTN_BLOCK, width), lambda b, g, i: (b, i, g)),
        out_shape=jax.ShapeDtypeStruct((batch, seq_len, d), BF16),
        compiler_params=pltpu.CompilerParams(
            dimension_semantics=("parallel", "parallel", "arbitrary"), vmem_limit_bytes=VMEM_LIMIT),
        name="diff_attention",
    )(lam_params, q3, kv3, kv3, subln_g.reshape(1, LANES))
    return out.reshape(batch * seq_len, d)


def _post_attn_kernel(h_ref, o_ref, wo_ref, g_ref, win_ref, wout_ref, out_ref):
    h1 = h_ref[...] + _dot(o_ref[...], wo_ref[...])
    hn = (h1 * _rms_scale(h1) * g_ref[...]).astype(BF16)
    acc = h1
    d_ff = win_ref.shape[1]
    for c0 in range(0, d_ff, FF_CHUNK):
        a = jnp.maximum(_dot(hn, win_ref[:, c0:c0 + FF_CHUNK]), 0.0)
        acc = acc + _dot((a * a).astype(BF16), wout_ref[c0:c0 + FF_CHUNK, :])
    out_ref[...] = acc


def _post_attention(h, o, wo, g, w_in, w_out, *, tm):
    n, d = h.shape
    row = pl.BlockSpec((tm, d), lambda i: (i, 0))
    return pl.pallas_call(
        _post_attn_kernel,
        grid=(n // tm,),
        in_specs=[row, pl.BlockSpec((tm, o.shape[1]), lambda i: (i, 0)), _resident(wo.shape),
                  _resident((1, d)), _resident(w_in.shape), _resident(w_out.shape)],
        out_specs=row,
        out_shape=jax.ShapeDtypeStruct((n, d), F32),
        compiler_params=pltpu.CompilerParams(
            dimension_semantics=("parallel",), vmem_limit_bytes=VMEM_LIMIT),
        name="post_attention",
    )(h, o, wo, g.reshape(1, d), w_in, w_out)


def _rope_tables(seq_len):
    half = ROPE_DIM // 2
    inv_freq = jnp.power(ROPE_THETA, -jnp.arange(0, ROPE_DIM, 2, dtype=F32) / ROPE_DIM)
    ang = jnp.arange(seq_len, dtype=F32)[:, None] * inv_freq[None, :]
    cos, sin = jnp.cos(ang), jnp.sin(ang)
    seg_lane = jnp.arange(LANES) % HEAD_DIM
    idx = seg_lane % half
    cos_t = jnp.where(seg_lane < ROPE_DIM, cos[:, idx], 1.0)
    s1_t = jnp.where(seg_lane < half, -sin[:, idx], 0.0)
    s2_t = jnp.where((seg_lane >= half) & (seg_lane < ROPE_DIM), sin[:, idx], 0.0)
    return cos_t, s1_t, s2_t


def kernel(x, meta_tokens, a_norm_g, a_w_qkv, a_w_o, kv_norm_g, kv_w, kv_k_norm_g, b_norm_g, b_w_q,
           b_q_norm_g, b_lambda, b_subln_g, b_w_o, mlp_norm_g, mlp_w_in, mlp_w_out):
    batch, s_len, d = x.shape
    n_a = a_w_qkv.shape[0]
    n_b = b_w_q.shape[0]
    real = s_len + N_META
    seq_len = -(-real // ATTN_BLOCK) * ATTN_BLOCK
    n_rows = batch * seq_len
    tm = ROW_TILE if n_rows % ROW_TILE == 0 else ATTN_BLOCK

    meta = jnp.broadcast_to(meta_tokens.astype(x.dtype)[None], (batch, N_META, d))
    h = jnp.concatenate([meta, x, jnp.zeros((batch, seq_len - real, d), x.dtype)], axis=1)
    h = h.reshape(n_rows, d)
    rope = _rope_tables(seq_len)
    k_cols = DIFF_HEADS * 2 * HEAD_DIM
    qk_scale = HEAD_DIM ** -0.5 * math.log2(math.e)

    kv = None
    for layer in range(n_a + n_b):
        if layer < n_a:
            i = layer
            col_scale = jnp.concatenate([jnp.full((d,), qk_scale, F32), jnp.ones((2 * d,), F32)])
            w_qkv = (a_w_qkv[i] * col_scale).astype(BF16)
            qkv = _norm_proj(h, a_norm_g[i], w_qkv, tm=tm)
            o = _sb_attention(qkv, batch, seq_len)
            w_o = a_w_o[i]
        else:
            j = layer - n_a
            if kv is None:
                k_gain = jnp.tile(kv_k_norm_g.reshape(2 * HEAD_DIM), DIFF_HEADS)
                kv = _norm_proj(h, kv_norm_g, kv_w.astype(BF16), tm=ATTN_BLOCK, seg_gain=k_gain,
                                rope=rope, n_rope_cols=k_cols, seq_len=seq_len)
            lam_init = 0.8 - 0.6 * math.exp(-0.3 * layer)
            q_gain = jnp.tile(b_q_norm_g[j].reshape(2 * HEAD_DIM), DIFF_HEADS) * qk_scale
            q = _norm_proj(h, b_norm_g[j], b_w_q[j].astype(BF16), tm=ATTN_BLOCK, seg_gain=q_gain,
                           rope=rope, n_rope_cols=k_cols, seq_len=seq_len)
            o = _diff_attention(q, kv, b_lambda[j].astype(F32), b_subln_g[j], lam_init, batch, seq_len)
            w_o = b_w_o[j]
        h = _post_attention(h, o, w_o.astype(BF16), mlp_norm_g[layer], mlp_w_in[layer].astype(BF16),
                            mlp_w_out[layer].astype(BF16), tm=tm)

    return h.reshape(batch, seq_len, d)[:, N_META:N_META + s_len]
```

```python
import functools
import math

import jax
import jax.numpy as jnp
from jax import lax
from jax.experimental import pallas as pl
from jax.experimental.pallas import tpu as pltpu

N_META = 16
SB_HEADS = 16
HEAD_DIM = 64
DIFF_HEADS = 8
ROPE_DIM = 16
ROPE_THETA = 500000.0
EPS = 1e-6

LANES = 128
ATTN_BLOCK = 256
ROW_TILE = 512
FF_CHUNK = 1024
SB_PAIRS_PER_STEP = 4
DIFF_HEADS_PER_STEP = 4
VMEM_LIMIT = 56 * 1024 * 1024

F32 = jnp.float32
BF16 = jnp.bfloat16
_NT = (((1,), (1,)), ((), ()))
_TN = (((0,), (0,)), ((), ()))


def _dot(a, b):
    return jnp.dot(a, b, preferred_element_type=F32)


def _rms_scale(x):
    return lax.rsqrt(jnp.mean(x * x, axis=-1, keepdims=True) + EPS)


def _resident(shape):
    nd = len(shape)
    return pl.BlockSpec(shape, lambda *_: (0,) * nd, pipeline_mode=pl.Buffered(1))


def _norm_proj_kernel(h_ref, g_ref, w_ref, *rest, n_rope_cols):
    if n_rope_cols:
        sg_ref, cos_ref, s1_ref, s2_ref, o_ref = rest
    else:
        (o_ref,) = rest
    x = h_ref[...]
    hn = (x * _rms_scale(x) * g_ref[...]).astype(BF16)
    n_out = o_ref.shape[1]
    if n_rope_cols:
        r = lax.broadcasted_iota(jnp.int32, (LANES, LANES), 0) // HEAD_DIM
        c = lax.broadcasted_iota(jnp.int32, (LANES, LANES), 1) // HEAD_DIM
        seg = jnp.where(r == c, 1.0, 0.0).astype(BF16)
        cos, s1, s2 = cos_ref[...], s1_ref[...], s2_ref[...]
    chunk = min(FF_CHUNK, n_out)
    for c0 in range(0, n_out, chunk):
        y = _dot(hn, w_ref[:, c0:c0 + chunk])
        if c0 >= n_rope_cols:
            o_ref[:, c0:c0 + chunk] = y.astype(o_ref.dtype)
            continue
        for l0 in range(0, chunk, LANES):
            yc = y[:, l0:l0 + LANES]
            sq = yc * yc
            hi = sq.astype(BF16)
            lo = (sq - hi.astype(F32)).astype(BF16)
            ms = (_dot(hi, seg) + _dot(lo, seg)) * (1.0 / HEAD_DIM)
            yn = yc * lax.rsqrt(ms + EPS) * sg_ref[:, c0 + l0:c0 + l0 + LANES]
            rot = (yn * cos
                   + pltpu.roll(yn, LANES - ROPE_DIM // 2, 1) * s1
                   + pltpu.roll(yn, ROPE_DIM // 2, 1) * s2)
            o_ref[:, c0 + l0:c0 + l0 + LANES] = rot.astype(o_ref.dtype)


def _norm_proj(h, g, w, *, tm, seg_gain=None, rope=None, n_rope_cols=0, seq_len=None):
    n, d = h.shape
    n_out = w.shape[1]
    in_specs = [pl.BlockSpec((tm, d), lambda i: (i, 0)), _resident((1, d)), _resident((d, n_out))]
    args = [h, g.reshape(1, d), w]
    if n_rope_cols:
        tiles_per_seq = seq_len // tm
        tab = pl.BlockSpec((tm, LANES), lambda i: (i % tiles_per_seq, 0))
        in_specs += [_resident((1, n_rope_cols)), tab, tab, tab]
        args += [seg_gain.reshape(1, n_rope_cols), *rope]
    return pl.pallas_call(
        functools.partial(_norm_proj_kernel, n_rope_cols=n_rope_cols),
        grid=(n // tm,),
        in_specs=in_specs,
        out_specs=pl.BlockSpec((tm, n_out), lambda i: (i, 0)),
        out_shape=jax.ShapeDtypeStruct((n, n_out), BF16),
        compiler_params=pltpu.CompilerParams(
            dimension_semantics=("parallel",), vmem_limit_bytes=VMEM_LIMIT),
        name="norm_proj_rope" if n_rope_cols else "norm_proj",
    )(*args)


def _stack_halves(q, lo_half):
    zero = jnp.zeros_like(q)
    return jnp.concatenate([jnp.where(lo_half, q, zero), jnp.where(lo_half, zero, q)], axis=0)


def _neg_abs(x):
    sign = jnp.uint32(0x80000000)
    return lax.bitcast_convert_type(lax.bitcast_convert_type(x, jnp.uint32) | sign, F32)


def _sb_attn_kernel(q_ref, k_ref, v_ref, o_ref, *, n_pairs):
    qi = pl.program_id(2)
    blk = q_ref.shape[1]
    pairs = range(n_pairs)
    lo_half = lax.broadcasted_iota(jnp.int32, (1, LANES), 1) < HEAD_DIM
    qs = [_stack_halves(q_ref[0, :, p * LANES:(p + 1) * LANES], lo_half) for p in pairs]
    row = lax.broadcasted_iota(jnp.int32, (2 * blk, blk), 0)
    col = lax.broadcasted_iota(jnp.int32, (2 * blk, blk), 1)
    key_j = lax.broadcasted_iota(jnp.int32, (blk, blk), 0)
    key_s = lax.broadcasted_iota(jnp.int32, (blk, blk), 1)
    neg_tri = jnp.where(key_j >= key_s, -1.0, 0.0).astype(BF16)
    past = col < row % blk

    def tile(j, state, masked):
        carry, acc = state
        start = pl.multiple_of(j * blk, blk)
        kb = [k_ref[0, pl.ds(start, blk), p * LANES:(p + 1) * LANES] for p in pairs]
        vb = [v_ref[0, pl.ds(start, blk), p * LANES:(p + 1) * LANES] for p in pairs]
        y = [lax.dot_general(qs[p], kb[p], _NT, preferred_element_type=F32) for p in pairs]
        split = []
        for p in pairs:
            sp = jnp.maximum(y[p], 0.0) + jnp.log2(1.0 + jnp.exp2(_neg_abs(y[p])))
            if masked:
                sp = jnp.where(past, sp, 0.0)
            split.append(sp.astype(BF16))
        cum = [_dot(split[p], neg_tri) for p in pairs]
        new_acc = []
        for p in pairs:
            w = jnp.exp2(y[p] + cum[p] + carry[p])
            if masked:
                w = jnp.where(past, w, 0.0)
            w = w.astype(BF16)
            zero = jnp.zeros_like(vb[p])
            v2 = jnp.concatenate([jnp.where(lo_half, vb[p], zero), jnp.where(lo_half, zero, vb[p])], axis=0)
            new_acc.append(acc[p] + _dot(jnp.concatenate([w[:blk], w[blk:]], axis=1), v2))
        return tuple(carry[p] + cum[p][:, 0:1] for p in pairs), tuple(new_acc)

    init = (tuple(jnp.zeros((2 * blk, 1), F32) for _ in pairs),
            tuple(jnp.zeros((blk, LANES), F32) for _ in pairs))
    state = tile(qi, init, True)
    _, acc = lax.fori_loop(0, qi, lambda i, st: tile(qi - 1 - i, st, False), state)
    for p in pairs:
        o_ref[0, :, p * LANES:(p + 1) * LANES] = acc[p].astype(o_ref.dtype)


def _sb_attention(qkv, batch, seq_len):
    d = SB_HEADS * HEAD_DIM
    width = SB_PAIRS_PER_STEP * LANES
    groups = d // width
    qkv3 = qkv.reshape(batch, seq_len, 3 * d)
    out = pl.pallas_call(
        functools.partial(_sb_attn_kernel, n_pairs=SB_PAIRS_PER_STEP),
        grid=(batch, groups, seq_len // ATTN_BLOCK),
        in_specs=[
            pl.BlockSpec((1, ATTN_BLOCK, width), lambda b, g, i: (b, i, g)),
            pl.BlockSpec((1, seq_len, width), lambda b, g, i: (b, 0, groups + g)),
            pl.BlockSpec((1, seq_len, width), lambda b, g, i: (b, 0, 2 * groups + g)),
        ],
        out_specs=pl.BlockSpec((1, ATTN_BLOCK, width), lambda b, g, i: (b, i, g)),
        out_shape=jax.ShapeDtypeStruct((batch, seq_len, d), BF16),
        compiler_params=pltpu.CompilerParams(
            dimension_semantics=("parallel", "parallel", "arbitrary"), vmem_limit_bytes=VMEM_LIMIT),
        name="sb_attention",
    )(qkv3, qkv3, qkv3)
    return out.reshape(batch * seq_len, d)


def _diff_attn_kernel(lam_ref, q_ref, k_ref, v_ref, g_ref, o_ref, *, n_heads, lam_init):
    qi = pl.program_id(2)
    blk = q_ref.shape[1]
    heads = range(n_heads)
    lo_half = lax.broadcasted_iota(jnp.int32, (1, LANES), 1) < HEAD_DIM
    qs = [_stack_halves(q_ref[0, :, h * LANES:(h + 1) * LANES], lo_half) for h in heads]
    key = lax.broadcasted_iota(jnp.int32, (blk, 2 * blk), 0)
    qry = lax.broadcasted_iota(jnp.int32, (blk, 2 * blk), 1)
    causal = key <= qry % blk

    def tile(j, state, masked):
        start = pl.multiple_of(j * blk, blk)
        kb = [k_ref[0, pl.ds(start, blk), h * LANES:(h + 1) * LANES] for h in heads]
        vb = [v_ref[0, pl.ds(start, blk), h * LANES:(h + 1) * LANES] for h in heads]
        s = [lax.dot_general(kb[h], qs[h], _NT, preferred_element_type=F32) for h in heads]
        m_new, l_new, p, alpha = [], [], [], []
        for h in heads:
            sh = jnp.where(causal, s[h], -jnp.inf) if masked else s[h]
            s_max = jnp.max(sh, axis=0, keepdims=True)
            if state is None:
                m_new.append(s_max)
                ph = jnp.exp2(sh - s_max)
                l_new.append(jnp.sum(ph, axis=0, keepdims=True))
            else:
                m, l, _ = state[h]
                m_new.append(jnp.maximum(m, s_max))
                alpha.append(jnp.exp2(m - m_new[h]))
                ph = jnp.exp2(sh - m_new[h])
                l_new.append(alpha[h] * l + jnp.sum(ph, axis=0, keepdims=True))
            p.append(ph.astype(BF16))
        pv = [lax.dot_general(vb[h], p[h], _TN, preferred_element_type=F32) for h in heads]
        if state is None:
            return tuple((m_new[h], l_new[h], pv[h]) for h in heads)
        return tuple((m_new[h], l_new[h], alpha[h] * state[h][2] + pv[h]) for h in heads)

    state = tile(qi, None, True)
    state = lax.fori_loop(0, qi, lambda j, st: tile(j, st, False), state)

    lp = lam_ref[...]
    lam = (jnp.exp(jnp.sum(lp[0:1] * lp[1:2], axis=-1, keepdims=True))
           - jnp.exp(jnp.sum(lp[2:3] * lp[3:4], axis=-1, keepdims=True)) + lam_init)
    for h in heads:
        _, l, acc = state[h]
        o = acc * (1.0 / l)
        o = o[:, :blk] - lam * o[:, blk:]
        ms = jnp.mean(o * o, axis=0, keepdims=True)
        o = o * lax.rsqrt(ms + EPS) * (g_ref[...] * (1.0 - lam_init))
        o_ref[0, :, h * LANES:(h + 1) * LANES] = o.T.astype(o_ref.dtype)


def _diff_attention(q, kv, lam_params, subln_g, lam_init, batch, seq_len):
    d = DIFF_HEADS * LANES
    width = DIFF_HEADS_PER_STEP * LANES
    groups = d // width
    q3 = q.reshape(batch, seq_len, d)
    kv3 = kv.reshape(batch, seq_len, 2 * d)
    out = pl.pallas_call(
        functools.partial(_diff_attn_kernel, n_heads=DIFF_HEADS_PER_STEP, lam_init=lam_init),
        grid=(batch, groups, seq_len // ATTN_BLOCK),
        in_specs=[
            pl.BlockSpec(lam_params.shape, lambda b, g, i: (0, 0)),
            pl.BlockSpec((1, ATTN_BLOCK, width), lambda b, g, i: (b, i, g)),
            pl.BlockSpec((1, seq_len, width), lambda b, g, i: (b, 0, g)),
            pl.BlockSpec((1, seq_len, width), lambda b, g, i: (b, 0, groups + g)),
            pl.BlockSpec((LANES, 1), lambda b, g, i: (0, 0)),
        ],
        out_specs=pl.BlockSpec((1, ATTN_BLOCK, width), lambda b, g, i: (b, i, g)),
        out_shape=jax.ShapeDtypeStruct((batch, seq_len, d), BF16),
        compiler_params=pltpu.CompilerParams(
            dimension_semantics=("parallel", "parallel", "arbitrary"), vmem_limit_bytes=VMEM_LIMIT),
        name="diff_attention",
    )(lam_params, q3, kv3, kv3, subln_g.reshape(LANES, 1))
    return out.reshape(batch * seq_len, d)


def _post_attn_kernel(h_ref, o_ref, wo_ref, g_ref, win_ref, wout_ref, out_ref):
    h1 = h_ref[...] + _dot(o_ref[...], wo_ref[...])
    hn = (h1 * _rms_scale(h1) * g_ref[...]).astype(BF16)
    acc = h1
    d_ff = win_ref.shape[1]
    for c0 in range(0, d_ff, FF_CHUNK):
        a = jnp.maximum(_dot(hn, win_ref[:, c0:c0 + FF_CHUNK]), 0.0)
        acc = acc + _dot((a * a).astype(BF16), wout_ref[c0:c0 + FF_CHUNK, :])
    out_ref[...] = acc


def _post_attention(h, o, wo, g, w_in, w_out, *, tm):
    n, d = h.shape
    row = pl.BlockSpec((tm, d), lambda i: (i, 0))
    return pl.pallas_call(
        _post_attn_kernel,
        grid=(n // tm,),
        in_specs=[row, pl.BlockSpec((tm, o.shape[1]), lambda i: (i, 0)), _resident(wo.shape),
                  _resident((1, d)), _resident(w_in.shape), _resident(w_out.shape)],
        out_specs=row,
        out_shape=jax.ShapeDtypeStruct((n, d), F32),
        compiler_params=pltpu.CompilerParams(
            dimension_semantics=("parallel",), vmem_limit_bytes=VMEM_LIMIT),
        name="post_attention",
    )(h, o, wo, g.reshape(1, d), w_in, w_out)


def _rope_tables(seq_len):
    half = ROPE_DIM // 2
    inv_freq = jnp.power(ROPE_THETA, -jnp.arange(0, ROPE_DIM, 2, dtype=F32) / ROPE_DIM)
    ang = jnp.arange(seq_len, dtype=F32)[:, None] * inv_freq[None, :]
    cos, sin = jnp.cos(ang), jnp.sin(ang)
    seg_lane = jnp.arange(LANES) % HEAD_DIM
    idx = seg_lane % half
    cos_t = jnp.where(seg_lane < ROPE_DIM, cos[:, idx], 1.0)
    s1_t = jnp.where(seg_lane < half, -sin[:, idx], 0.0)
    s2_t = jnp.where((seg_lane >= half) & (seg_lane < ROPE_DIM), sin[:, idx], 0.0)
    return cos_t, s1_t, s2_t


def kernel(x, meta_tokens, a_norm_g, a_w_qkv, a_w_o, kv_norm_g, kv_w, kv_k_norm_g, b_norm_g, b_w_q,
           b_q_norm_g, b_lambda, b_subln_g, b_w_o, mlp_norm_g, mlp_w_in, mlp_w_out):
    batch, s_len, d = x.shape
    n_a = a_w_qkv.shape[0]
    n_b = b_w_q.shape[0]
    real = s_len + N_META
    seq_len = -(-real // ATTN_BLOCK) * ATTN_BLOCK
    n_rows = batch * seq_len
    tm = ROW_TILE if n_rows % ROW_TILE == 0 else ATTN_BLOCK

    meta = jnp.broadcast_to(meta_tokens.astype(x.dtype)[None], (batch, N_META, d))
    h = jnp.concatenate([meta, x, jnp.zeros((batch, seq_len - real, d), x.dtype)], axis=1)
    h = h.reshape(n_rows, d)
    rope = _rope_tables(seq_len)
    k_cols = DIFF_HEADS * 2 * HEAD_DIM
    qk_scale = HEAD_DIM ** -0.5 * math.log2(math.e)

    kv = None
    for layer in range(n_a + n_b):
        if layer < n_a:
            i = layer
            col_scale = jnp.concatenate([jnp.full((d,), qk_scale, F32), jnp.ones((2 * d,), F32)])
            w_qkv = (a_w_qkv[i] * col_scale).astype(BF16)
            qkv = _norm_proj(h, a_norm_g[i], w_qkv, tm=tm)
            o = _sb_attention(qkv, batch, seq_len)
            w_o = a_w_o[i]
        else:
            j = layer - n_a
            if kv is None:
                k_gain = jnp.tile(kv_k_norm_g.reshape(2 * HEAD_DIM), DIFF_HEADS)
                kv = _norm_proj(h, kv_norm_g, kv_w.astype(BF16), tm=ATTN_BLOCK, seg_gain=k_gain,
                                rope=rope, n_rope_cols=k_cols, seq_len=seq_len)
            lam_init = 0.8 - 0.6 * math.exp(-0.3 * layer)
            q_gain = jnp.tile(b_q_norm_g[j].reshape(2 * HEAD_DIM), DIFF_HEADS) * qk_scale
            q = _norm_proj(h, b_norm_g[j], b_w_q[j].astype(BF16), tm=ATTN_BLOCK, seg_gain=q_gain,
                           rope=rope, n_rope_cols=k_cols, seq_len=seq_len)
            o = _diff_attention(q, kv, b_lambda[j].astype(F32), b_subln_g[j], lam_init, batch, seq_len)
            w_o = b_w_o[j]
        h = _post_attention(h, o, w_o.astype(BF16), mlp_norm_g[layer], mlp_w_in[layer].astype(BF16),
                            mlp_w_out[layer].astype(BF16), tm=tm)

    return h.reshape(batch, seq_len, d)[:, N_META:N_META + s_len]
```

```python
import functools
import math

import jax
import jax.numpy as jnp
from jax import lax
from jax.experimental import pallas as pl
from jax.experimental.pallas import tpu as pltpu

N_META = 16
SB_HEADS = 16
HEAD_DIM = 64
DIFF_HEADS = 8
ROPE_DIM = 16
ROPE_THETA = 500000.0
EPS = 1e-6

LANES = 128
ATTN_BLOCK = 256
ROW_TILE = 512
FF_CHUNK = 1024
SB_PAIRS_PER_STEP = 4
DIFF_HEADS_PER_STEP = 4
VMEM_LIMIT = 56 * 1024 * 1024

F32 = jnp.float32
BF16 = jnp.bfloat16
_NT = (((1,), (1,)), ((), ()))
_TN = (((0,), (0,)), ((), ()))


def _dot(a, b):
    return jnp.dot(a, b, preferred_element_type=F32)


def _rms_scale(x):
    return lax.rsqrt(jnp.mean(x * x, axis=-1, keepdims=True) + EPS)


def _resident(shape):
    nd = len(shape)
    return pl.BlockSpec(shape, lambda *_: (0,) * nd, pipeline_mode=pl.Buffered(1))


def _norm_proj_kernel(h_ref, g_ref, w_ref, *rest, n_rope_cols):
    if n_rope_cols:
        sg_ref, cos_ref, s1_ref, s2_ref, o_ref = rest
    else:
        (o_ref,) = rest
    x = h_ref[...]
    hn = (x * _rms_scale(x) * g_ref[...]).astype(BF16)
    n_out = o_ref.shape[1]
    if n_rope_cols:
        r = lax.broadcasted_iota(jnp.int32, (LANES, LANES), 0) // HEAD_DIM
        c = lax.broadcasted_iota(jnp.int32, (LANES, LANES), 1) // HEAD_DIM
        seg = jnp.where(r == c, 1.0, 0.0).astype(BF16)
        cos, s1, s2 = cos_ref[...], s1_ref[...], s2_ref[...]
    chunk = min(FF_CHUNK, n_out)
    for c0 in range(0, n_out, chunk):
        y = _dot(hn, w_ref[:, c0:c0 + chunk])
        if c0 >= n_rope_cols:
            o_ref[:, c0:c0 + chunk] = y.astype(o_ref.dtype)
            continue
        for l0 in range(0, chunk, LANES):
            yc = y[:, l0:l0 + LANES]
            sq = yc * yc
            hi = sq.astype(BF16)
            lo = (sq - hi.astype(F32)).astype(BF16)
            ms = (_dot(hi, seg) + _dot(lo, seg)) * (1.0 / HEAD_DIM)
            yn = yc * lax.rsqrt(ms + EPS) * sg_ref[:, c0 + l0:c0 + l0 + LANES]
            rot = (yn * cos
                   + pltpu.roll(yn, LANES - ROPE_DIM // 2, 1) * s1
                   + pltpu.roll(yn, ROPE_DIM // 2, 1) * s2)
            o_ref[:, c0 + l0:c0 + l0 + LANES] = rot.astype(o_ref.dtype)


def _norm_proj(h, g, w, *, tm, seg_gain=None, rope=None, n_rope_cols=0, seq_len=None):
    n, d = h.shape
    n_out = w.shape[1]
    in_specs = [pl.BlockSpec((tm, d), lambda i: (i, 0)), _resident((1, d)), _resident((d, n_out))]
    args = [h, g.reshape(1, d), w]
    if n_rope_cols:
        tiles_per_seq = seq_len // tm
        tab = pl.BlockSpec((tm, LANES), lambda i: (i % tiles_per_seq, 0))
        in_specs += [_resident((1, n_rope_cols)), tab, tab, tab]
        args += [seg_gain.reshape(1, n_rope_cols), *rope]
    return pl.pallas_call(
        functools.partial(_norm_proj_kernel, n_rope_cols=n_rope_cols),
        grid=(n // tm,),
        in_specs=in_specs,
        out_specs=pl.BlockSpec((tm, n_out), lambda i: (i, 0)),
        out_shape=jax.ShapeDtypeStruct((n, n_out), BF16),
        compiler_params=pltpu.CompilerParams(
            dimension_semantics=("parallel",), vmem_limit_bytes=VMEM_LIMIT),
        name="norm_proj_rope" if n_rope_cols else "norm_proj",
    )(*args)


def _stack_halves(q, lo_half):
    zero = jnp.zeros_like(q)
    return jnp.concatenate([jnp.where(lo_half, q, zero), jnp.where(lo_half, zero, q)], axis=0)


def _neg_abs(x):
    sign = jnp.uint32(0x80000000)
    return lax.bitcast_convert_type(lax.bitcast_convert_type(x, jnp.uint32) | sign, F32)


def _sb_attn_kernel(q_ref, k_ref, v_ref, o_ref, y_a, y_b, carry_ref, acc_ref, *, n_pairs):
    qi = pl.program_id(2)
    n_blocks = qi + 1
    blk = q_ref.shape[1]
    pairs = range(n_pairs)
    lo_half = lax.broadcasted_iota(jnp.int32, (1, LANES), 1) < HEAD_DIM
    qs = [_stack_halves(q_ref[0, :, p * LANES:(p + 1) * LANES], lo_half) for p in pairs]
    row = lax.broadcasted_iota(jnp.int32, (2 * blk, blk), 0)
    col = lax.broadcasted_iota(jnp.int32, (2 * blk, blk), 1)
    key_j = lax.broadcasted_iota(jnp.int32, (blk, blk), 0)
    key_s = lax.broadcasted_iota(jnp.int32, (blk, blk), 1)
    neg_tri = jnp.where(key_j >= key_s, -1.0, 0.0).astype(BF16)
    past = col < row % blk

    def key_start(t):
        return pl.multiple_of((qi - t) * blk, blk)

    def logits(t, y_ref):
        start = key_start(t)
        for p in pairs:
            kb = k_ref[0, pl.ds(start, blk), p * LANES:(p + 1) * LANES]
            y_ref[p] = lax.dot_general(qs[p], kb, _NT, preferred_element_type=F32)

    def absorb(t, y_ref, masked):
        start = key_start(t)
        vb = [v_ref[0, pl.ds(start, blk), p * LANES:(p + 1) * LANES] for p in pairs]
        sps = []
        for p in pairs:
            y = y_ref[p]
            sp = jnp.maximum(y, 0.0) + jnp.log2(1.0 + jnp.exp2(_neg_abs(y)))
            if masked:
                sp = jnp.where(past, sp, 0.0)
            sps.append(sp.astype(BF16))
        cum = [_dot(sps[p], neg_tri) for p in pairs]
        for p in pairs:
            w = jnp.exp2(y_ref[p] + cum[p] + carry_ref[p])
            if masked:
                w = jnp.where(past, w, 0.0)
            w = w.astype(BF16)
            acc_ref[p, 0] += _dot(w[:blk], vb[p])
            acc_ref[p, 1] += _dot(w[blk:], vb[p])
            carry_ref[p] += cum[p][:, 0:1]

    carry_ref[...] = jnp.zeros_like(carry_ref)
    acc_ref[...] = jnp.zeros_like(acc_ref)
    logits(0, y_a)

    @pl.when(n_blocks == 1)
    def _():
        absorb(0, y_a, True)

    @pl.when(n_blocks > 1)
    def _():
        logits(1, y_b)
        absorb(0, y_a, True)

        def two_blocks(k, _):
            t = 2 * k + 1
            logits(t + 1, y_a)
            absorb(t, y_b, False)
            logits(t + 2, y_b)
            absorb(t + 1, y_a, False)
            return 0

        lax.fori_loop(0, (n_blocks - 2) // 2, two_blocks, 0)

        @pl.when(n_blocks % 2 == 0)
        def _():
            absorb(n_blocks - 1, y_b, False)

        @pl.when(n_blocks % 2 == 1)
        def _():
            logits(n_blocks - 1, y_a)
            absorb(n_blocks - 2, y_b, False)
            absorb(n_blocks - 1, y_a, False)

    for p in pairs:
        o_ref[0, :, p * LANES:(p + 1) * LANES] = jnp.where(
            lo_half, acc_ref[p, 0], acc_ref[p, 1]).astype(o_ref.dtype)


def _sb_attention(qkv, batch, seq_len):
    d = SB_HEADS * HEAD_DIM
    width = SB_PAIRS_PER_STEP * LANES
    groups = d // width
    qkv3 = qkv.reshape(batch, seq_len, 3 * d)
    out = pl.pallas_call(
        functools.partial(_sb_attn_kernel, n_pairs=SB_PAIRS_PER_STEP),
        grid=(batch, groups, seq_len // ATTN_BLOCK),
        in_specs=[
            pl.BlockSpec((1, ATTN_BLOCK, width), lambda b, g, i: (b, i, g)),
            pl.BlockSpec((1, seq_len, width), lambda b, g, i: (b, 0, groups + g)),
            pl.BlockSpec((1, seq_len, width), lambda b, g, i: (b, 0, 2 * groups + g)),
        ],
        out_specs=pl.BlockSpec((1, ATTN_BLOCK, width), lambda b, g, i: (b, i, g)),
        out_shape=jax.ShapeDtypeStruct((batch, seq_len, d), BF16),
        scratch_shapes=[pltpu.VMEM((SB_PAIRS_PER_STEP, 2 * ATTN_BLOCK, ATTN_BLOCK), F32)] * 2 + [
            pltpu.VMEM((SB_PAIRS_PER_STEP, 2 * ATTN_BLOCK, 1), F32),
            pltpu.VMEM((SB_PAIRS_PER_STEP, 2, ATTN_BLOCK, LANES), F32)],
        compiler_params=pltpu.CompilerParams(
            dimension_semantics=("parallel", "parallel", "arbitrary"), vmem_limit_bytes=VMEM_LIMIT),
        name="sb_attention",
    )(qkv3, qkv3, qkv3)
    return out.reshape(batch * seq_len, d)


def _diff_attn_kernel(lam_ref, q_ref, k_ref, v_ref, g_ref, o_ref, s_a, s_b, m_ref, l_ref, acc_ref, *,
                      n_heads, lam_init):
    qi = pl.program_id(2)
    blk = q_ref.shape[1]
    heads = range(n_heads)
    lo_half = lax.broadcasted_iota(jnp.int32, (1, LANES), 1) < HEAD_DIM
    qs = [_stack_halves(q_ref[0, :, h * LANES:(h + 1) * LANES], lo_half) for h in heads]
    key = lax.broadcasted_iota(jnp.int32, (blk, 2 * blk), 0)
    qry = lax.broadcasted_iota(jnp.int32, (blk, 2 * blk), 1)
    causal = key <= qry % blk

    def scores(t, s_ref):
        start = pl.multiple_of(t * blk, blk)
        for h in heads:
            kb = k_ref[0, pl.ds(start, blk), h * LANES:(h + 1) * LANES]
            s_ref[h] = lax.dot_general(kb, qs[h], _NT, preferred_element_type=F32)

    def absorb(t, s_ref, masked):
        start = pl.multiple_of(t * blk, blk)
        vb = [v_ref[0, pl.ds(start, blk), h * LANES:(h + 1) * LANES] for h in heads]
        p, alpha = [], []
        for h in heads:
            sh = s_ref[h]
            if masked:
                sh = jnp.where(causal, sh, -jnp.inf)
            m = m_ref[h]
            m_new = jnp.maximum(m, jnp.max(sh, axis=0, keepdims=True))
            alpha.append(jnp.exp2(m - m_new))
            ph = jnp.exp2(sh - m_new)
            l_ref[h] = alpha[h] * l_ref[h] + jnp.sum(ph, axis=0, keepdims=True)
            m_ref[h] = m_new
            p.append(ph.astype(BF16))
        pv = [lax.dot_general(vb[h], p[h], _TN, preferred_element_type=F32) for h in heads]
        for h in heads:
            acc_ref[h] = alpha[h] * acc_ref[h] + pv[h]

    m_ref[...] = jnp.full_like(m_ref, -jnp.inf)
    l_ref[...] = jnp.zeros_like(l_ref)
    acc_ref[...] = jnp.zeros_like(acc_ref)
    scores(0, s_a)

    def two_blocks(k, _):
        t = 2 * k
        scores(t + 1, s_b)
        absorb(t, s_a, False)
        scores(t + 2, s_a)
        absorb(t + 1, s_b, False)
        return 0

    lax.fori_loop(0, qi // 2, two_blocks, 0)

    @pl.when(qi % 2 == 0)
    def _():
        absorb(qi, s_a, True)

    @pl.when(qi % 2 == 1)
    def _():
        scores(qi, s_b)
        absorb(qi - 1, s_a, False)
        absorb(qi, s_b, True)

    lp = lam_ref[...]
    lam = (jnp.exp(jnp.sum(lp[0:1] * lp[1:2], axis=-1, keepdims=True))
           - jnp.exp(jnp.sum(lp[2:3] * lp[3:4], axis=-1, keepdims=True)) + lam_init)
    for h in heads:
        o = acc_ref[h] * (1.0 / l_ref[h])
        o = o[:, :blk] - lam * o[:, blk:]
        ms = jnp.mean(o * o, axis=0, keepdims=True)
        o = o * lax.rsqrt(ms + EPS) * (g_ref[...] * (1.0 - lam_init))
        o_ref[0, :, h * LANES:(h + 1) * LANES] = o.T.astype(o_ref.dtype)


def _diff_attention(q, kv, lam_params, subln_g, lam_init, batch, seq_len):
    d = DIFF_HEADS * LANES
    width = DIFF_HEADS_PER_STEP * LANES
    groups = d // width
    q3 = q.reshape(batch, seq_len, d)
    kv3 = kv.reshape(batch, seq_len, 2 * d)
    out = pl.pallas_call(
        functools.partial(_diff_attn_kernel, n_heads=DIFF_HEADS_PER_STEP, lam_init=lam_init),
        grid=(batch, groups, seq_len // ATTN_BLOCK),
        in_specs=[
            pl.BlockSpec(lam_params.shape, lambda b, g, i: (0, 0)),
            pl.BlockSpec((1, ATTN_BLOCK, width), lambda b, g, i: (b, i, g)),
            pl.BlockSpec((1, seq_len, width), lambda b, g, i: (b, 0, g)),
            pl.BlockSpec((1, seq_len, width), lambda b, g, i: (b, 0, groups + g)),
            pl.BlockSpec((LANES, 1), lambda b, g, i: (0, 0)),
        ],
        out_specs=pl.BlockSpec((1, ATTN_BLOCK, width), lambda b, g, i: (b, i, g)),
        out_shape=jax.ShapeDtypeStruct((batch, seq_len, d), BF16),
        scratch_shapes=[pltpu.VMEM((DIFF_HEADS_PER_STEP, ATTN_BLOCK, 2 * ATTN_BLOCK), F32)] * 2 + [
            pltpu.VMEM((DIFF_HEADS_PER_STEP, 1, 2 * ATTN_BLOCK), F32)] * 2 + [
            pltpu.VMEM((DIFF_HEADS_PER_STEP, LANES, 2 * ATTN_BLOCK), F32)],
        compiler_params=pltpu.CompilerParams(
            dimension_semantics=("parallel", "parallel", "arbitrary"), vmem_limit_bytes=VMEM_LIMIT),
        name="diff_attention",
    )(lam_params, q3, kv3, kv3, subln_g.reshape(LANES, 1))
    return out.reshape(batch * seq_len, d)


def _post_attn_kernel(h_ref, o_ref, wo_ref, g_ref, win_ref, wout_ref, out_ref):
    h1 = h_ref[...] + _dot(o_ref[...], wo_ref[...])
    hn = (h1 * _rms_scale(h1) * g_ref[...]).astype(BF16)
    acc = h1
    d_ff = win_ref.shape[1]
    for c0 in range(0, d_ff, FF_CHUNK):
        a = jnp.maximum(_dot(hn, win_ref[:, c0:c0 + FF_CHUNK]), 0.0)
        acc = acc + _dot((a * a).astype(BF16), wout_ref[c0:c0 + FF_CHUNK, :])
    out_ref[...] = acc


def _post_attention(h, o, wo, g, w_in, w_out, *, tm):
    n, d = h.shape
    row = pl.BlockSpec((tm, d), lambda i: (i, 0))
    return pl.pallas_call(
        _post_attn_kernel,
        grid=(n // tm,),
        in_specs=[row, pl.BlockSpec((tm, o.shape[1]), lambda i: (i, 0)), _resident(wo.shape),
                  _resident((1, d)), _resident(w_in.shape), _resident(w_out.shape)],
        out_specs=row,
        out_shape=jax.ShapeDtypeStruct((n, d), F32),
        compiler_params=pltpu.CompilerParams(
            dimension_semantics=("parallel",), vmem_limit_bytes=VMEM_LIMIT),
        name="post_attention",
    )(h, o, wo, g.reshape(1, d), w_in, w_out)


def _rope_tables(seq_len):
    half = ROPE_DIM // 2
    inv_freq = jnp.power(ROPE_THETA, -jnp.arange(0, ROPE_DIM, 2, dtype=F32) / ROPE_DIM)
    ang = jnp.arange(seq_len, dtype=F32)[:, None] * inv_freq[None, :]
    cos, sin = jnp.cos(ang), jnp.sin(ang)
    seg_lane = jnp.arange(LANES) % HEAD_DIM
    idx = seg_lane % half
    cos_t = jnp.where(seg_lane < ROPE_DIM, cos[:, idx], 1.0)
    s1_t = jnp.where(seg_lane < half, -sin[:, idx], 0.0)
    s2_t = jnp.where((seg_lane >= half) & (seg_lane < ROPE_DIM), sin[:, idx], 0.0)
    return cos_t, s1_t, s2_t


def kernel(x, meta_tokens, a_norm_g, a_w_qkv, a_w_o, kv_norm_g, kv_w, kv_k_norm_g, b_norm_g, b_w_q,
           b_q_norm_g, b_lambda, b_subln_g, b_w_o, mlp_norm_g, mlp_w_in, mlp_w_out):
    batch, s_len, d = x.shape
    n_a = a_w_qkv.shape[0]
    n_b = b_w_q.shape[0]
    real = s_len + N_META
    seq_len = -(-real // ATTN_BLOCK) * ATTN_BLOCK
    n_rows = batch * seq_len
    tm = ROW_TILE if n_rows % ROW_TILE == 0 else ATTN_BLOCK

    meta = jnp.broadcast_to(meta_tokens.astype(x.dtype)[None], (batch, N_META, d))
    h = jnp.concatenate([meta, x, jnp.zeros((batch, seq_len - real, d), x.dtype)], axis=1)
    h = h.reshape(n_rows, d)
    rope = _rope_tables(seq_len)
    k_cols = DIFF_HEADS * 2 * HEAD_DIM
    qk_scale = HEAD_DIM ** -0.5 * math.log2(math.e)

    kv = None
    for layer in range(n_a + n_b):
        if layer < n_a:
            i = layer
            col_scale = jnp.concatenate([jnp.full((d,), qk_scale, F32), jnp.ones((2 * d,), F32)])
            w_qkv = (a_w_qkv[i] * col_scale).astype(BF16)
            qkv = _norm_proj(h, a_norm_g[i], w_qkv, tm=tm)
            o = _sb_attention(qkv, batch, seq_len)
            w_o = a_w_o[i]
        else:
            j = layer - n_a
            if kv is None:
                k_gain = jnp.tile(kv_k_norm_g.reshape(2 * HEAD_DIM), DIFF_HEADS)
                kv = _norm_proj(h, kv_norm_g, kv_w.astype(BF16), tm=ATTN_BLOCK, seg_gain=k_gain,
                                rope=rope, n_rope_cols=k_cols, seq_len=seq_len)
            lam_init = 0.8 - 0.6 * math.exp(-0.3 * layer)
            q_gain = jnp.tile(b_q_norm_g[j].reshape(2 * HEAD_DIM), DIFF_HEADS) * qk_scale
            q = _norm_proj(h, b_norm_g[j], b_w_q[j].astype(BF16), tm=ATTN_BLOCK, seg_gain=q_gain,
                           rope=rope, n_rope_cols=k_cols, seq_len=seq_len)
            o = _diff_attention(q, kv, b_lambda[j].astype(F32), b_subln_g[j], lam_init, batch, seq_len)
            w_o = b_w_o[j]
        h = _post_attention(h, o, w_o.astype(BF16), mlp_norm_g[layer], mlp_w_in[layer].astype(BF16),
                            mlp_w_out[layer].astype(BF16), tm=tm)

    return h.reshape(batch, seq_len, d)[:, N_META:N_META + s_len]
```

```python
import functools
import math

import jax
import jax.numpy as jnp
from jax import lax
from jax.experimental import pallas as pl
from jax.experimental.pallas import tpu as pltpu

N_META = 16
SB_HEADS = 16
HEAD_DIM = 64
DIFF_HEADS = 8
ROPE_DIM = 16
ROPE_THETA = 500000.0
EPS = 1e-6

LANES = 128
ATTN_BLOCK = 256
ROW_TILE = 512
FF_CHUNK = 1024
SB_PAIRS_PER_STEP = 4
DIFF_HEADS_PER_STEP = 4
VMEM_LIMIT = 56 * 1024 * 1024

F32 = jnp.float32
BF16 = jnp.bfloat16
_NT = (((1,), (1,)), ((), ()))
_TN = (((0,), (0,)), ((), ()))


def _dot(a, b):
    return jnp.dot(a, b, preferred_element_type=F32)


def _rms_scale(x):
    return lax.rsqrt(jnp.mean(x * x, axis=-1, keepdims=True) + EPS)


def _resident(shape):
    nd = len(shape)
    return pl.BlockSpec(shape, lambda *_: (0,) * nd, pipeline_mode=pl.Buffered(1))


def _norm_proj_kernel(h_ref, g_ref, w_ref, *rest, n_rope_cols):
    if n_rope_cols:
        sg_ref, cos_ref, s1_ref, s2_ref, o_ref = rest
    else:
        (o_ref,) = rest
    x = h_ref[...]
    hn = (x * _rms_scale(x) * g_ref[...]).astype(BF16)
    n_out = o_ref.shape[1]
    if n_rope_cols:
        r = lax.broadcasted_iota(jnp.int32, (LANES, LANES), 0) // HEAD_DIM
        c = lax.broadcasted_iota(jnp.int32, (LANES, LANES), 1) // HEAD_DIM
        seg = jnp.where(r == c, 1.0, 0.0).astype(BF16)
        cos, s1, s2 = cos_ref[...], s1_ref[...], s2_ref[...]
    chunk = min(FF_CHUNK, n_out)
    for c0 in range(0, n_out, chunk):
        y = _dot(hn, w_ref[:, c0:c0 + chunk])
        if c0 >= n_rope_cols:
            o_ref[:, c0:c0 + chunk] = y.astype(o_ref.dtype)
            continue
        for l0 in range(0, chunk, LANES):
            yc = y[:, l0:l0 + LANES]
            sq = yc * yc
            hi = sq.astype(BF16)
            lo = (sq - hi.astype(F32)).astype(BF16)
            ms = (_dot(hi, seg) + _dot(lo, seg)) * (1.0 / HEAD_DIM)
            yn = yc * lax.rsqrt(ms + EPS) * sg_ref[:, c0 + l0:c0 + l0 + LANES]
            rot = (yn * cos
                   + pltpu.roll(yn, LANES - ROPE_DIM // 2, 1) * s1
                   + pltpu.roll(yn, ROPE_DIM // 2, 1) * s2)
            o_ref[:, c0 + l0:c0 + l0 + LANES] = rot.astype(o_ref.dtype)


def _norm_proj(h, g, w, *, tm, seg_gain=None, rope=None, n_rope_cols=0, seq_len=None):
    n, d = h.shape
    n_out = w.shape[1]
    in_specs = [pl.BlockSpec((tm, d), lambda i: (i, 0)), _resident((1, d)), _resident((d, n_out))]
    args = [h, g.reshape(1, d), w]
    if n_rope_cols:
        tiles_per_seq = seq_len // tm
        tab = pl.BlockSpec((tm, LANES), lambda i: (i % tiles_per_seq, 0))
        in_specs += [_resident((1, n_rope_cols)), tab, tab, tab]
        args += [seg_gain.reshape(1, n_rope_cols), *rope]
    return pl.pallas_call(
        functools.partial(_norm_proj_kernel, n_rope_cols=n_rope_cols),
        grid=(n // tm,),
        in_specs=in_specs,
        out_specs=pl.BlockSpec((tm, n_out), lambda i: (i, 0)),
        out_shape=jax.ShapeDtypeStruct((n, n_out), BF16),
        compiler_params=pltpu.CompilerParams(
            dimension_semantics=("parallel",), vmem_limit_bytes=VMEM_LIMIT),
        name="norm_proj_rope" if n_rope_cols else "norm_proj",
    )(*args)


def _stack_halves(q, lo_half):
    zero = jnp.zeros_like(q)
    return jnp.concatenate([jnp.where(lo_half, q, zero), jnp.where(lo_half, zero, q)], axis=0)


def _neg_abs(x):
    sign = jnp.uint32(0x80000000)
    return lax.bitcast_convert_type(lax.bitcast_convert_type(x, jnp.uint32) | sign, F32)


def _all_of(*masks):
    masks = [m for m in masks if m is not None]
    return functools.reduce(jnp.logical_and, masks) if masks else None


def _visit_blocks(n_steps, produce, consume, buf_a, buf_b, first, last, only):
    @pl.when(n_steps == 1)
    def _():
        produce(0, buf_a, only)
        consume(0, buf_a)

    @pl.when(n_steps > 1)
    def _():
        produce(0, buf_a, first)

        def two_steps(k, _):
            t = 2 * k
            produce(t + 1, buf_b, None)
            consume(t, buf_a)
            produce(t + 2, buf_a, None)
            consume(t + 1, buf_b)
            return 0

        lax.fori_loop(0, (n_steps - 2) // 2, two_steps, 0)

        @pl.when(n_steps % 2 == 0)
        def _():
            produce(n_steps - 1, buf_b, last)
            consume(n_steps - 2, buf_a)
            consume(n_steps - 1, buf_b)

        @pl.when(n_steps % 2 == 1)
        def _():
            produce(n_steps - 2, buf_b, None)
            consume(n_steps - 3, buf_a)
            produce(n_steps - 1, buf_a, last)
            consume(n_steps - 2, buf_b)
            consume(n_steps - 1, buf_a)


def _sb_attn_kernel(q_ref, k_ref, v_ref, o_ref, y_a, y_b, carry_ref, acc_ref, *, n_pairs, n_pad):
    qi = pl.program_id(2)
    blk = q_ref.shape[1]
    pairs = range(n_pairs)
    lo_half = lax.broadcasted_iota(jnp.int32, (1, LANES), 1) < HEAD_DIM
    qs = [_stack_halves(q_ref[0, :, p * LANES:(p + 1) * LANES], lo_half) for p in pairs]
    row = lax.broadcasted_iota(jnp.int32, (2 * blk, blk), 0)
    col = lax.broadcasted_iota(jnp.int32, (2 * blk, blk), 1)
    key_j = lax.broadcasted_iota(jnp.int32, (blk, blk), 0)
    key_s = lax.broadcasted_iota(jnp.int32, (blk, blk), 1)
    neg_tri = jnp.where(key_j >= key_s, -1.0, 0.0).astype(BF16)
    past = col < row % blk
    real_key = col >= n_pad if n_pad else None

    def logits(t, y_ref, mask):
        start = pl.multiple_of((qi - t) * blk, blk)
        for p in pairs:
            kb = k_ref[0, pl.ds(start, blk), p * LANES:(p + 1) * LANES]
            y = lax.dot_general(qs[p], kb, _NT, preferred_element_type=F32)
            y_ref[p] = y if mask is None else jnp.where(mask, y, -jnp.inf)

    def absorb(t, y_ref):
        start = pl.multiple_of((qi - t) * blk, blk)
        vb = [v_ref[0, pl.ds(start, blk), p * LANES:(p + 1) * LANES] for p in pairs]
        sps = []
        for p in pairs:
            y = y_ref[p]
            sp = jnp.maximum(y, 0.0) + jnp.log2(1.0 + jnp.exp2(_neg_abs(y)))
            sps.append(sp.astype(BF16))
        cum = [_dot(sps[p], neg_tri) for p in pairs]
        for p in pairs:
            w = jnp.exp2(y_ref[p] + cum[p] + carry_ref[p]).astype(BF16)
            acc_ref[p, 0] += _dot(w[:blk], vb[p])
            acc_ref[p, 1] += _dot(w[blk:], vb[p])
            carry_ref[p] += cum[p][:, 0:1]

    carry_ref[...] = jnp.zeros_like(carry_ref)
    acc_ref[...] = jnp.zeros_like(acc_ref)
    _visit_blocks(qi + 1, logits, absorb, y_a, y_b, first=past, last=real_key, only=_all_of(past, real_key))
    for p in pairs:
        o_ref[0, :, p * LANES:(p + 1) * LANES] = jnp.where(
            lo_half, acc_ref[p, 0], acc_ref[p, 1]).astype(o_ref.dtype)


def _sb_attention(qkv, batch, seq_len, n_pad):
    d = SB_HEADS * HEAD_DIM
    width = SB_PAIRS_PER_STEP * LANES
    groups = d // width
    qkv3 = qkv.reshape(batch, seq_len, 3 * d)
    out = pl.pallas_call(
        functools.partial(_sb_attn_kernel, n_pairs=SB_PAIRS_PER_STEP, n_pad=n_pad),
        grid=(batch, groups, seq_len // ATTN_BLOCK),
        in_specs=[
            pl.BlockSpec((1, ATTN_BLOCK, width), lambda b, g, i: (b, i, g)),
            pl.BlockSpec((1, seq_len, width), lambda b, g, i: (b, 0, groups + g)),
            pl.BlockSpec((1, seq_len, width), lambda b, g, i: (b, 0, 2 * groups + g)),
        ],
        out_specs=pl.BlockSpec((1, ATTN_BLOCK, width), lambda b, g, i: (b, i, g)),
        out_shape=jax.ShapeDtypeStruct((batch, seq_len, d), BF16),
        scratch_shapes=[pltpu.VMEM((SB_PAIRS_PER_STEP, 2 * ATTN_BLOCK, ATTN_BLOCK), F32)] * 2 + [
            pltpu.VMEM((SB_PAIRS_PER_STEP, 2 * ATTN_BLOCK, 1), F32),
            pltpu.VMEM((SB_PAIRS_PER_STEP, 2, ATTN_BLOCK, LANES), F32)],
        compiler_params=pltpu.CompilerParams(
            dimension_semantics=("parallel", "parallel", "arbitrary"), vmem_limit_bytes=VMEM_LIMIT),
        name="sb_attention",
    )(qkv3, qkv3, qkv3)
    return out.reshape(batch * seq_len, d)


def _diff_attn_kernel(lam_ref, q_ref, k_ref, v_ref, g_ref, o_ref, s_a, s_b, m_ref, l_ref, acc_ref, *,
                      n_heads, n_pad, lam_init):
    qi = pl.program_id(2)
    blk = q_ref.shape[1]
    heads = range(n_heads)
    lo_half = lax.broadcasted_iota(jnp.int32, (1, LANES), 1) < HEAD_DIM
    qs = [_stack_halves(q_ref[0, :, h * LANES:(h + 1) * LANES], lo_half) for h in heads]
    key = lax.broadcasted_iota(jnp.int32, (blk, 2 * blk), 0)
    qry = lax.broadcasted_iota(jnp.int32, (blk, 2 * blk), 1) % blk
    causal = key <= qry
    real_key = key >= n_pad if n_pad else None
    only = _all_of(causal, jnp.logical_or(real_key, qry < n_pad)) if n_pad else causal

    def scores(t, s_ref, mask):
        start = pl.multiple_of(t * blk, blk)
        for h in heads:
            kb = k_ref[0, pl.ds(start, blk), h * LANES:(h + 1) * LANES]
            sc = lax.dot_general(kb, qs[h], _NT, preferred_element_type=F32)
            s_ref[h] = sc if mask is None else jnp.where(mask, sc, -jnp.inf)

    def absorb(t, s_ref):
        start = pl.multiple_of(t * blk, blk)
        vb = [v_ref[0, pl.ds(start, blk), h * LANES:(h + 1) * LANES] for h in heads]
        p, alpha = [], []
        for h in heads:
            sh = s_ref[h]
            m = m_ref[h]
            m_new = jnp.maximum(m, jnp.max(sh, axis=0, keepdims=True))
            alpha.append(jnp.exp2(m - m_new))
            ph = jnp.exp2(sh - m_new)
            l_ref[h] = alpha[h] * l_ref[h] + jnp.sum(ph, axis=0, keepdims=True)
            m_ref[h] = m_new
            p.append(ph.astype(BF16))
        pv = [lax.dot_general(vb[h], p[h], _TN, preferred_element_type=F32) for h in heads]
        for h in heads:
            acc_ref[h] = alpha[h] * acc_ref[h] + pv[h]

    m_ref[...] = jnp.full_like(m_ref, -jnp.inf)
    l_ref[...] = jnp.zeros_like(l_ref)
    acc_ref[...] = jnp.zeros_like(acc_ref)
    _visit_blocks(qi + 1, scores, absorb, s_a, s_b, first=real_key, last=causal, only=only)

    lp = lam_ref[...]
    lam = (jnp.exp(jnp.sum(lp[0:1] * lp[1:2], axis=-1, keepdims=True))
           - jnp.exp(jnp.sum(lp[2:3] * lp[3:4], axis=-1, keepdims=True)) + lam_init)
    for h in heads:
        o = acc_ref[h] * (1.0 / l_ref[h])
        o = o[:, :blk] - lam * o[:, blk:]
        ms = jnp.mean(o * o, axis=0, keepdims=True)
        o = o * lax.rsqrt(ms + EPS) * (g_ref[...] * (1.0 - lam_init))
        o_ref[0, :, h * LANES:(h + 1) * LANES] = o.T.astype(o_ref.dtype)


def _diff_attention(q, kv, lam_params, subln_g, lam_init, batch, seq_len, n_pad):
    d = DIFF_HEADS * LANES
    width = DIFF_HEADS_PER_STEP * LANES
    groups = d // width
    q3 = q.reshape(batch, seq_len, d)
    kv3 = kv.reshape(batch, seq_len, 2 * d)
    out = pl.pallas_call(
        functools.partial(_diff_attn_kernel, n_heads=DIFF_HEADS_PER_STEP, n_pad=n_pad, lam_init=lam_init),
        grid=(batch, groups, seq_len // ATTN_BLOCK),
        in_specs=[
            pl.BlockSpec(lam_params.shape, lambda b, g, i: (0, 0)),
            pl.BlockSpec((1, ATTN_BLOCK, width), lambda b, g, i: (b, i, g)),
            pl.BlockSpec((1, seq_len, width), lambda b, g, i: (b, 0, g)),
            pl.BlockSpec((1, seq_len, width), lambda b, g, i: (b, 0, groups + g)),
            pl.BlockSpec((LANES, 1), lambda b, g, i: (0, 0)),
        ],
        out_specs=pl.BlockSpec((1, ATTN_BLOCK, width), lambda b, g, i: (b, i, g)),
        out_shape=jax.ShapeDtypeStruct((batch, seq_len, d), BF16),
        scratch_shapes=[pltpu.VMEM((DIFF_HEADS_PER_STEP, ATTN_BLOCK, 2 * ATTN_BLOCK), F32)] * 2 + [
            pltpu.VMEM((DIFF_HEADS_PER_STEP, 1, 2 * ATTN_BLOCK), F32)] * 2 + [
            pltpu.VMEM((DIFF_HEADS_PER_STEP, LANES, 2 * ATTN_BLOCK), F32)],
        compiler_params=pltpu.CompilerParams(
            dimension_semantics=("parallel", "parallel", "arbitrary"), vmem_limit_bytes=VMEM_LIMIT),
        name="diff_attention",
    )(lam_params, q3, kv3, kv3, subln_g.reshape(LANES, 1))
    return out.reshape(batch * seq_len, d)


def _post_attn_kernel(h_ref, o_ref, wo_ref, g_ref, win_ref, wout_ref, out_ref):
    h1 = h_ref[...] + _dot(o_ref[...], wo_ref[...])
    hn = (h1 * _rms_scale(h1) * g_ref[...]).astype(BF16)
    acc = h1
    d_ff = win_ref.shape[1]
    for c0 in range(0, d_ff, FF_CHUNK):
        a = jnp.maximum(_dot(hn, win_ref[:, c0:c0 + FF_CHUNK]), 0.0)
        acc = acc + _dot((a * a).astype(BF16), wout_ref[c0:c0 + FF_CHUNK, :])
    out_ref[...] = acc


def _post_attention(h, o, wo, g, w_in, w_out, *, tm):
    n, d = h.shape
    row = pl.BlockSpec((tm, d), lambda i: (i, 0))
    return pl.pallas_call(
        _post_attn_kernel,
        grid=(n // tm,),
        in_specs=[row, pl.BlockSpec((tm, o.shape[1]), lambda i: (i, 0)), _resident(wo.shape),
                  _resident((1, d)), _resident(w_in.shape), _resident(w_out.shape)],
        out_specs=row,
        out_shape=jax.ShapeDtypeStruct((n, d), F32),
        compiler_params=pltpu.CompilerParams(
            dimension_semantics=("parallel",), vmem_limit_bytes=VMEM_LIMIT),
        name="post_attention",
    )(h, o, wo, g.reshape(1, d), w_in, w_out)


def _rope_tables(seq_len, n_pad):
    half = ROPE_DIM // 2
    inv_freq = jnp.power(ROPE_THETA, -jnp.arange(0, ROPE_DIM, 2, dtype=F32) / ROPE_DIM)
    pos = jnp.maximum(jnp.arange(seq_len) - n_pad, 0).astype(F32)
    ang = pos[:, None] * inv_freq[None, :]
    cos, sin = jnp.cos(ang), jnp.sin(ang)
    seg_lane = jnp.arange(LANES) % HEAD_DIM
    idx = seg_lane % half
    cos_t = jnp.where(seg_lane < ROPE_DIM, cos[:, idx], 1.0)
    s1_t = jnp.where(seg_lane < half, -sin[:, idx], 0.0)
    s2_t = jnp.where((seg_lane >= half) & (seg_lane < ROPE_DIM), sin[:, idx], 0.0)
    return cos_t, s1_t, s2_t


def kernel(x, meta_tokens, a_norm_g, a_w_qkv, a_w_o, kv_norm_g, kv_w, kv_k_norm_g, b_norm_g, b_w_q,
           b_q_norm_g, b_lambda, b_subln_g, b_w_o, mlp_norm_g, mlp_w_in, mlp_w_out):
    batch, s_len, d = x.shape
    n_a = a_w_qkv.shape[0]
    n_b = b_w_q.shape[0]
    real = s_len + N_META
    seq_len = -(-real // ATTN_BLOCK) * ATTN_BLOCK
    n_pad = seq_len - real
    n_rows = batch * seq_len
    tm = ROW_TILE if n_rows % ROW_TILE == 0 else ATTN_BLOCK

    meta = jnp.broadcast_to(meta_tokens.astype(x.dtype)[None], (batch, N_META, d))
    h = jnp.concatenate([jnp.zeros((batch, n_pad, d), x.dtype), meta, x], axis=1)
    h = h.reshape(n_rows, d)
    rope = _rope_tables(seq_len, n_pad)
    k_cols = DIFF_HEADS * 2 * HEAD_DIM
    qk_scale = HEAD_DIM ** -0.5 * math.log2(math.e)

    kv = None
    for layer in range(n_a + n_b):
        if layer < n_a:
            i = layer
            col_scale = jnp.concatenate([jnp.full((d,), qk_scale, F32), jnp.ones((2 * d,), F32)])
            w_qkv = (a_w_qkv[i] * col_scale).astype(BF16)
            qkv = _norm_proj(h, a_norm_g[i], w_qkv, tm=tm)
            o = _sb_attention(qkv, batch, seq_len, n_pad)
            w_o = a_w_o[i]
        else:
            j = layer - n_a
            if kv is None:
                k_gain = jnp.tile(kv_k_norm_g.reshape(2 * HEAD_DIM), DIFF_HEADS)
                kv = _norm_proj(h, kv_norm_g, kv_w.astype(BF16), tm=ATTN_BLOCK, seg_gain=k_gain,
                                rope=rope, n_rope_cols=k_cols, seq_len=seq_len)
            lam_init = 0.8 - 0.6 * math.exp(-0.3 * layer)
            q_gain = jnp.tile(b_q_norm_g[j].reshape(2 * HEAD_DIM), DIFF_HEADS) * qk_scale
            q = _norm_proj(h, b_norm_g[j], b_w_q[j].astype(BF16), tm=ATTN_BLOCK, seg_gain=q_gain,
                           rope=rope, n_rope_cols=k_cols, seq_len=seq_len)
            o = _diff_attention(q, kv, b_lambda[j].astype(F32), b_subln_g[j], lam_init, batch, seq_len, n_pad)
            w_o = b_w_o[j]
        h = _post_attention(h, o, w_o.astype(BF16), mlp_norm_g[layer], mlp_w_in[layer].astype(BF16),
                            mlp_w_out[layer].astype(BF16), tm=tm)

    return h.reshape(batch, seq_len, d)[:, n_pad + N_META:]
```

```python
import functools
import math

import jax
import jax.numpy as jnp
from jax import lax
from jax.experimental import pallas as pl
from jax.experimental.pallas import tpu as pltpu

N_META = 16
SB_HEADS = 16
HEAD_DIM = 64
DIFF_HEADS = 8
ROPE_DIM = 16
ROPE_THETA = 500000.0
EPS = 1e-6

LANES = 128
ATTN_BLOCK = 256
ROW_TILE = 512
FF_CHUNK = 1024
SB_PAIRS_PER_STEP = 4
DIFF_HEADS_PER_STEP = 4
VMEM_LIMIT = 56 * 1024 * 1024

F32 = jnp.float32
BF16 = jnp.bfloat16
_NT = (((1,), (1,)), ((), ()))


def _dot(a, b):
    return jnp.dot(a, b, preferred_element_type=F32)


def _rms_scale(x):
    return lax.rsqrt(jnp.mean(x * x, axis=-1, keepdims=True) + EPS)


def _resident(shape):
    nd = len(shape)
    return pl.BlockSpec(shape, lambda *_: (0,) * nd, pipeline_mode=pl.Buffered(1))


def _norm_proj_kernel(h_ref, g_ref, w_ref, *rest, n_rope_cols, transposed_extra):
    wt_ref = ot_ref = None
    if transposed_extra:
        wt_ref, *rest, ot_ref = rest
    if n_rope_cols:
        sg_ref, cos_ref, s1_ref, s2_ref, o_ref = rest
    else:
        (o_ref,) = rest
    x = h_ref[...]
    hn = (x * _rms_scale(x) * g_ref[...]).astype(BF16)
    n_out = o_ref.shape[1]
    if n_rope_cols:
        r = lax.broadcasted_iota(jnp.int32, (LANES, LANES), 0) // HEAD_DIM
        c = lax.broadcasted_iota(jnp.int32, (LANES, LANES), 1) // HEAD_DIM
        seg = jnp.where(r == c, 1.0, 0.0).astype(BF16)
        cos, s1, s2 = cos_ref[...], s1_ref[...], s2_ref[...]
    chunk = min(FF_CHUNK, n_out)
    for c0 in range(0, n_out, chunk):
        y = _dot(hn, w_ref[:, c0:c0 + chunk])
        if c0 >= n_rope_cols:
            o_ref[:, c0:c0 + chunk] = y.astype(o_ref.dtype)
            continue
        for l0 in range(0, chunk, LANES):
            yc = y[:, l0:l0 + LANES]
            ms = _dot((yc * yc).astype(BF16), seg) * (1.0 / HEAD_DIM)
            yn = yc * lax.rsqrt(ms + EPS) * sg_ref[:, c0 + l0:c0 + l0 + LANES]
            rot = (yn * cos
                   + pltpu.roll(yn, LANES - ROPE_DIM // 2, 1) * s1
                   + pltpu.roll(yn, ROPE_DIM // 2, 1) * s2)
            o_ref[:, c0 + l0:c0 + l0 + LANES] = rot.astype(o_ref.dtype)
    if transposed_extra:
        ot_ref[0] = lax.dot_general(wt_ref[...], hn, _NT, preferred_element_type=F32).astype(ot_ref.dtype)


def _norm_proj(h, g, w, *, tm, seg_gain=None, rope=None, n_rope_cols=0, seq_len=None, w_t=None):
    n, d = h.shape
    n_out = w.shape[1]
    in_specs = [pl.BlockSpec((tm, d), lambda i: (i, 0)), _resident((1, d)), _resident((d, n_out))]
    args = [h, g.reshape(1, d), w]
    if w_t is not None:
        in_specs.append(_resident(w_t.shape))
        args.append(w_t)
    if n_rope_cols:
        tiles_per_seq = seq_len // tm
        tab = pl.BlockSpec((tm, LANES), lambda i: (i % tiles_per_seq, 0))
        in_specs += [_resident((1, n_rope_cols)), tab, tab, tab]
        args += [seg_gain.reshape(1, n_rope_cols), *rope]
    out_specs = pl.BlockSpec((tm, n_out), lambda i: (i, 0))
    out_shape = jax.ShapeDtypeStruct((n, n_out), BF16)
    if w_t is not None:
        out_specs = [out_specs, pl.BlockSpec((1, w_t.shape[0], tm), lambda i: (i, 0, 0))]
        out_shape = [out_shape, jax.ShapeDtypeStruct((n // tm, w_t.shape[0], tm), BF16)]
    return pl.pallas_call(
        functools.partial(_norm_proj_kernel, n_rope_cols=n_rope_cols, transposed_extra=w_t is not None),
        grid=(n // tm,),
        in_specs=in_specs,
        out_specs=out_specs,
        out_shape=out_shape,
        compiler_params=pltpu.CompilerParams(
            dimension_semantics=("parallel",), vmem_limit_bytes=VMEM_LIMIT),
        name="norm_proj_rope" if n_rope_cols else "norm_proj",
    )(*args)


def _stack_halves(q, lo_half):
    zero = jnp.zeros_like(q)
    return jnp.concatenate([jnp.where(lo_half, q, zero), jnp.where(lo_half, zero, q)], axis=0)


def _neg_abs(x):
    sign = jnp.uint32(0x80000000)
    return lax.bitcast_convert_type(lax.bitcast_convert_type(x, jnp.uint32) | sign, F32)


def _all_of(*masks):
    masks = [m for m in masks if m is not None]
    return functools.reduce(jnp.logical_and, masks) if masks else None


def _visit_blocks(n_steps, produce, consume, buf_a, buf_b, first, last, only):
    @pl.when(n_steps == 1)
    def _():
        produce(0, buf_a, only)
        consume(0, buf_a)

    @pl.when(n_steps > 1)
    def _():
        produce(0, buf_a, first)

        def two_steps(k, _):
            t = 2 * k
            produce(t + 1, buf_b, None)
            consume(t, buf_a)
            produce(t + 2, buf_a, None)
            consume(t + 1, buf_b)
            return 0

        lax.fori_loop(0, (n_steps - 2) // 2, two_steps, 0)

        @pl.when(n_steps % 2 == 0)
        def _():
            produce(n_steps - 1, buf_b, last)
            consume(n_steps - 2, buf_a)
            consume(n_steps - 1, buf_b)

        @pl.when(n_steps % 2 == 1)
        def _():
            produce(n_steps - 2, buf_b, None)
            consume(n_steps - 3, buf_a)
            produce(n_steps - 1, buf_a, last)
            consume(n_steps - 2, buf_b)
            consume(n_steps - 1, buf_a)


def _sb_attn_kernel(q_ref, k_ref, v_ref, o_ref, y_a, y_b, carry_ref, acc_ref, *, n_pairs, n_pad):
    qi = pl.program_id(2)
    blk = q_ref.shape[1]
    pairs = range(n_pairs)
    lo_half = lax.broadcasted_iota(jnp.int32, (1, LANES), 1) < HEAD_DIM
    qs = [_stack_halves(q_ref[0, :, p * LANES:(p + 1) * LANES], lo_half) for p in pairs]
    row = lax.broadcasted_iota(jnp.int32, (2 * blk, blk), 0)
    col = lax.broadcasted_iota(jnp.int32, (2 * blk, blk), 1)
    key_j = lax.broadcasted_iota(jnp.int32, (blk, blk), 0)
    key_s = lax.broadcasted_iota(jnp.int32, (blk, blk), 1)
    neg_tri = jnp.where(key_j >= key_s, -1.0, 0.0).astype(BF16)
    past = col < row % blk
    real_key = col >= n_pad if n_pad else None

    def logits(t, y_ref, mask):
        start = pl.multiple_of((qi - t) * blk, blk)
        for p in pairs:
            kb = k_ref[0, pl.ds(start, blk), p * LANES:(p + 1) * LANES]
            y = lax.dot_general(qs[p], kb, _NT, preferred_element_type=F32)
            y_ref[p] = y if mask is None else jnp.where(mask, y, -jnp.inf)

    def absorb(t, y_ref):
        start = pl.multiple_of((qi - t) * blk, blk)
        vb = [v_ref[0, pl.ds(start, blk), p * LANES:(p + 1) * LANES] for p in pairs]
        sps = []
        for p in pairs:
            y = y_ref[p]
            sp = jnp.maximum(y, 0.0) + jnp.log2(1.0 + jnp.exp2(_neg_abs(y)))
            sps.append(sp.astype(BF16))
        cum = [_dot(sps[p], neg_tri) for p in pairs]
        for p in pairs:
            carry = carry_ref[p]
            w = jnp.exp2(y_ref[p] + cum[p] + jnp.concatenate([carry, carry], axis=1)).astype(BF16)
            acc_ref[p, 0] += _dot(w[:blk], vb[p])
            acc_ref[p, 1] += _dot(w[blk:], vb[p])
            carry_ref[p] = carry + jnp.broadcast_to(cum[p][:, 0:1], carry.shape)

    carry_ref[...] = jnp.zeros_like(carry_ref)
    acc_ref[...] = jnp.zeros_like(acc_ref)
    _visit_blocks(qi + 1, logits, absorb, y_a, y_b, first=past, last=real_key, only=_all_of(past, real_key))
    for p in pairs:
        o_ref[0, :, p * LANES:(p + 1) * LANES] = jnp.where(
            lo_half, acc_ref[p, 0], acc_ref[p, 1]).astype(o_ref.dtype)


def _sb_attention(qkv, batch, seq_len, n_pad):
    d = SB_HEADS * HEAD_DIM
    width = SB_PAIRS_PER_STEP * LANES
    groups = d // width
    qkv3 = qkv.reshape(batch, seq_len, 3 * d)
    out = pl.pallas_call(
        functools.partial(_sb_attn_kernel, n_pairs=SB_PAIRS_PER_STEP, n_pad=n_pad),
        grid=(batch, groups, seq_len // ATTN_BLOCK),
        in_specs=[
            pl.BlockSpec((1, ATTN_BLOCK, width), lambda b, g, i: (b, i, g)),
            pl.BlockSpec((1, seq_len, width), lambda b, g, i: (b, 0, groups + g)),
            pl.BlockSpec((1, seq_len, width), lambda b, g, i: (b, 0, 2 * groups + g)),
        ],
        out_specs=pl.BlockSpec((1, ATTN_BLOCK, width), lambda b, g, i: (b, i, g)),
        out_shape=jax.ShapeDtypeStruct((batch, seq_len, d), BF16),
        scratch_shapes=[pltpu.VMEM((SB_PAIRS_PER_STEP, 2 * ATTN_BLOCK, ATTN_BLOCK), F32)] * 2 + [
            pltpu.VMEM((SB_PAIRS_PER_STEP, 2 * ATTN_BLOCK, LANES), F32),
            pltpu.VMEM((SB_PAIRS_PER_STEP, 2, ATTN_BLOCK, LANES), F32)],
        compiler_params=pltpu.CompilerParams(
            dimension_semantics=("parallel", "parallel", "arbitrary"), vmem_limit_bytes=VMEM_LIMIT),
        name="sb_attention",
    )(qkv3, qkv3, qkv3)
    return out.reshape(batch * seq_len, d)


def _diff_attn_kernel(lam_ref, q_ref, k_ref, vt_ref, g_ref, o_ref, s_a, s_b, m_ref, l_ref, acc_ref, *,
                      n_heads, n_pad, lam_init):
    qi = pl.program_id(2)
    blk = q_ref.shape[1]
    heads = range(n_heads)
    lo_half = lax.broadcasted_iota(jnp.int32, (1, LANES), 1) < HEAD_DIM
    qs = [_stack_halves(q_ref[0, :, h * LANES:(h + 1) * LANES], lo_half) for h in heads]
    key = lax.broadcasted_iota(jnp.int32, (blk, 2 * blk), 0)
    qry = lax.broadcasted_iota(jnp.int32, (blk, 2 * blk), 1) % blk
    causal = key <= qry
    real_key = key >= n_pad if n_pad else None
    only = _all_of(causal, jnp.logical_or(real_key, qry < n_pad)) if n_pad else causal

    def scores(t, s_ref, mask):
        start = pl.multiple_of(t * blk, blk)
        for h in heads:
            kb = k_ref[0, pl.ds(start, blk), h * LANES:(h + 1) * LANES]
            sc = lax.dot_general(kb, qs[h], _NT, preferred_element_type=F32)
            s_ref[h] = sc if mask is None else jnp.where(mask, sc, -jnp.inf)

    def absorb(t, s_ref):
        vt = [vt_ref[0, t, h * LANES:(h + 1) * LANES, :] for h in heads]
        p, alpha = [], []
        for h in heads:
            sh = s_ref[h]
            m = m_ref[h]
            m_new = jnp.maximum(m, jnp.max(sh, axis=0, keepdims=True))
            alpha.append(jnp.exp2(m - m_new))
            ph = jnp.exp2(sh - m_new)
            l_ref[h] = alpha[h] * l_ref[h] + jnp.sum(ph, axis=0, keepdims=True)
            m_ref[h] = m_new
            p.append(ph.astype(BF16))
        pv = [_dot(vt[h], p[h]) for h in heads]
        for h in heads:
            acc_ref[h] = alpha[h] * acc_ref[h] + pv[h]

    m_ref[...] = jnp.full_like(m_ref, -jnp.inf)
    l_ref[...] = jnp.zeros_like(l_ref)
    acc_ref[...] = jnp.zeros_like(acc_ref)
    _visit_blocks(qi + 1, scores, absorb, s_a, s_b, first=real_key, last=causal, only=only)

    lp = lam_ref[...]
    lam = (jnp.exp(jnp.sum(lp[0:1] * lp[1:2], axis=-1, keepdims=True))
           - jnp.exp(jnp.sum(lp[2:3] * lp[3:4], axis=-1, keepdims=True)) + lam_init)
    for h in heads:
        o = acc_ref[h] * (1.0 / l_ref[h])
        o = o[:, :blk] - lam * o[:, blk:]
        ms = jnp.mean(o * o, axis=0, keepdims=True)
        o = o * lax.rsqrt(ms + EPS) * (g_ref[...] * (1.0 - lam_init))
        o_ref[0, :, h * LANES:(h + 1) * LANES] = o.T.astype(o_ref.dtype)


def _diff_attention(q, k, vt, lam_params, subln_g, lam_init, batch, seq_len, n_pad):
    d = DIFF_HEADS * LANES
    width = DIFF_HEADS_PER_STEP * LANES
    groups = d // width
    n_blk = seq_len // ATTN_BLOCK
    q3 = q.reshape(batch, seq_len, d)
    k3 = k.reshape(batch, seq_len, d)
    vt4 = vt.reshape(batch, n_blk, d, ATTN_BLOCK)
    out = pl.pallas_call(
        functools.partial(_diff_attn_kernel, n_heads=DIFF_HEADS_PER_STEP, n_pad=n_pad, lam_init=lam_init),
        grid=(batch, groups, seq_len // ATTN_BLOCK),
        in_specs=[
            pl.BlockSpec(lam_params.shape, lambda b, g, i: (0, 0)),
            pl.BlockSpec((1, ATTN_BLOCK, width), lambda b, g, i: (b, i, g)),
            pl.BlockSpec((1, seq_len, width), lambda b, g, i: (b, 0, g)),
            pl.BlockSpec((1, n_blk, width, ATTN_BLOCK), lambda b, g, i: (b, 0, g, 0)),
            pl.BlockSpec((LANES, 1), lambda b, g, i: (0, 0)),
        ],
        out_specs=pl.BlockSpec((1, ATTN_BLOCK, width), lambda b, g, i: (b, i, g)),
        out_shape=jax.ShapeDtypeStruct((batch, seq_len, d), BF16),
        scratch_shapes=[pltpu.VMEM((DIFF_HEADS_PER_STEP, ATTN_BLOCK, 2 * ATTN_BLOCK), F32)] * 2 + [
            pltpu.VMEM((DIFF_HEADS_PER_STEP, 1, 2 * ATTN_BLOCK), F32)] * 2 + [
            pltpu.VMEM((DIFF_HEADS_PER_STEP, LANES, 2 * ATTN_BLOCK), F32)],
        compiler_params=pltpu.CompilerParams(
            dimension_semantics=("parallel", "parallel", "arbitrary"), vmem_limit_bytes=VMEM_LIMIT),
        name="diff_attention",
    )(lam_params, q3, k3, vt4, subln_g.reshape(LANES, 1))
    return out.reshape(batch * seq_len, d)


def _post_attn_kernel(h_ref, o_ref, wo_ref, g_ref, win_ref, wout_ref, out_ref):
    h1 = h_ref[...] + _dot(o_ref[...], wo_ref[...])
    hn = (h1 * _rms_scale(h1) * g_ref[...]).astype(BF16)
    acc = h1
    d_ff = win_ref.shape[1]
    for c0 in range(0, d_ff, FF_CHUNK):
        a = jnp.maximum(_dot(hn, win_ref[:, c0:c0 + FF_CHUNK]), 0.0)
        acc = acc + _dot((a * a).astype(BF16), wout_ref[c0:c0 + FF_CHUNK, :])
    out_ref[...] = acc


def _post_attention(h, o, wo, g, w_in, w_out, *, tm):
    n, d = h.shape
    row = pl.BlockSpec((tm, d), lambda i: (i, 0))
    return pl.pallas_call(
        _post_attn_kernel,
        grid=(n // tm,),
        in_specs=[row, pl.BlockSpec((tm, o.shape[1]), lambda i: (i, 0)), _resident(wo.shape),
                  _resident((1, d)), _resident(w_in.shape), _resident(w_out.shape)],
        out_specs=row,
        out_shape=jax.ShapeDtypeStruct((n, d), F32),
        compiler_params=pltpu.CompilerParams(
            dimension_semantics=("parallel",), vmem_limit_bytes=VMEM_LIMIT),
        name="post_attention",
    )(h, o, wo, g.reshape(1, d), w_in, w_out)


def _rope_tables(seq_len, n_pad):
    half = ROPE_DIM // 2
    inv_freq = jnp.power(ROPE_THETA, -jnp.arange(0, ROPE_DIM, 2, dtype=F32) / ROPE_DIM)
    pos = jnp.maximum(jnp.arange(seq_len) - n_pad, 0).astype(F32)
    ang = pos[:, None] * inv_freq[None, :]
    cos, sin = jnp.cos(ang), jnp.sin(ang)
    seg_lane = jnp.arange(LANES) % HEAD_DIM
    idx = seg_lane % half
    cos_t = jnp.where(seg_lane < ROPE_DIM, cos[:, idx], 1.0)
    s1_t = jnp.where(seg_lane < half, -sin[:, idx], 0.0)
    s2_t = jnp.where((seg_lane >= half) & (seg_lane < ROPE_DIM), sin[:, idx], 0.0)
    return cos_t, s1_t, s2_t


def kernel(x, meta_tokens, a_norm_g, a_w_qkv, a_w_o, kv_norm_g, kv_w, kv_k_norm_g, b_norm_g, b_w_q,
           b_q_norm_g, b_lambda, b_subln_g, b_w_o, mlp_norm_g, mlp_w_in, mlp_w_out):
    batch, s_len, d = x.shape
    n_a = a_w_qkv.shape[0]
    n_b = b_w_q.shape[0]
    real = s_len + N_META
    seq_len = -(-real // ATTN_BLOCK) * ATTN_BLOCK
    n_pad = seq_len - real
    n_rows = batch * seq_len
    tm = ROW_TILE if n_rows % ROW_TILE == 0 else ATTN_BLOCK

    meta = jnp.broadcast_to(meta_tokens.astype(x.dtype)[None], (batch, N_META, d))
    h = jnp.concatenate([jnp.zeros((batch, n_pad, d), x.dtype), meta, x], axis=1)
    h = h.reshape(n_rows, d)
    rope = _rope_tables(seq_len, n_pad)
    k_cols = DIFF_HEADS * 2 * HEAD_DIM
    qk_scale = HEAD_DIM ** -0.5 * math.log2(math.e)

    kv = None
    for layer in range(n_a + n_b):
        if layer < n_a:
            i = layer
            col_scale = jnp.concatenate([jnp.full((d,), qk_scale, F32), jnp.ones((2 * d,), F32)])
            w_qkv = (a_w_qkv[i] * col_scale).astype(BF16)
            qkv = _norm_proj(h, a_norm_g[i], w_qkv, tm=tm)
            o = _sb_attention(qkv, batch, seq_len, n_pad)
            w_o = a_w_o[i]
        else:
            j = layer - n_a
            if kv is None:
                k_gain = jnp.tile(kv_k_norm_g.reshape(2 * HEAD_DIM), DIFF_HEADS)
                kv = _norm_proj(h, kv_norm_g, kv_w[:, :k_cols].astype(BF16), tm=ATTN_BLOCK, seg_gain=k_gain,
                                rope=rope, n_rope_cols=k_cols, seq_len=seq_len,
                                w_t=kv_w[:, k_cols:].T.astype(BF16))
            lam_init = 0.8 - 0.6 * math.exp(-0.3 * layer)
            q_gain = jnp.tile(b_q_norm_g[j].reshape(2 * HEAD_DIM), DIFF_HEADS) * qk_scale
            q = _norm_proj(h, b_norm_g[j], b_w_q[j].astype(BF16), tm=ATTN_BLOCK, seg_gain=q_gain,
                           rope=rope, n_rope_cols=k_cols, seq_len=seq_len)
            o = _diff_attention(q, kv[0], kv[1], b_lambda[j].astype(F32), b_subln_g[j], lam_init, batch,
                                seq_len, n_pad)
            w_o = b_w_o[j]
        h = _post_attention(h, o, w_o.astype(BF16), mlp_norm_g[layer], mlp_w_in[layer].astype(BF16),
                            mlp_w_out[layer].astype(BF16), tm=tm)

    return h.reshape(batch, seq_len, d)[:, n_pad + N_META:]
```

```python
import functools
import math

import jax
import jax.numpy as jnp
from jax import lax
from jax.experimental import pallas as pl
from jax.experimental.pallas import tpu as pltpu

N_META = 16
SB_HEADS = 16
HEAD_DIM = 64
DIFF_HEADS = 8
ROPE_DIM = 16
ROPE_THETA = 500000.0
EPS = 1e-6

LANES = 128
ATTN_BLOCK = 256
ROW_TILE = 512
FF_CHUNK = 1024
SB_PAIRS_PER_STEP = 4
DIFF_HEADS_PER_STEP = 4
VMEM_LIMIT = 56 * 1024 * 1024

F32 = jnp.float32
BF16 = jnp.bfloat16
_NT = (((1,), (1,)), ((), ()))


def _dot(a, b):
    return jnp.dot(a, b, preferred_element_type=F32)


def _rms_scale(x):
    return lax.rsqrt(jnp.mean(x * x, axis=-1, keepdims=True) + EPS)


def _resident(shape):
    nd = len(shape)
    return pl.BlockSpec(shape, lambda *_: (0,) * nd, pipeline_mode=pl.Buffered(1))


def _norm_proj_kernel(h_ref, g_ref, w_ref, *rest, n_rope_cols, transposed_extra, transpose_out):
    wt_ref = ot_ref = None
    if transposed_extra:
        wt_ref, *rest, ot_ref = rest
    if n_rope_cols:
        sg_ref, cos_ref, s1_ref, s2_ref, o_ref = rest
    else:
        (o_ref,) = rest
    x = h_ref[...]
    hn = (x * _rms_scale(x) * g_ref[...]).astype(BF16)
    n_out = w_ref.shape[1]
    if n_rope_cols:
        r = lax.broadcasted_iota(jnp.int32, (LANES, LANES), 0) // HEAD_DIM
        c = lax.broadcasted_iota(jnp.int32, (LANES, LANES), 1) // HEAD_DIM
        seg = jnp.where(r == c, 1.0, 0.0).astype(BF16)
        cos, s1, s2 = cos_ref[...], s1_ref[...], s2_ref[...]
    chunk = min(FF_CHUNK, n_out)
    for c0 in range(0, n_out, chunk):
        y = _dot(hn, w_ref[:, c0:c0 + chunk])
        if c0 >= n_rope_cols:
            o_ref[:, c0:c0 + chunk] = y.astype(o_ref.dtype)
            continue
        for l0 in range(0, chunk, LANES):
            yc = y[:, l0:l0 + LANES]
            ms = _dot((yc * yc).astype(BF16), seg) * (1.0 / HEAD_DIM)
            yn = yc * lax.rsqrt(ms + EPS) * sg_ref[:, c0 + l0:c0 + l0 + LANES]
            rot = (yn * cos
                   + pltpu.roll(yn, LANES - ROPE_DIM // 2, 1) * s1
                   + pltpu.roll(yn, ROPE_DIM // 2, 1) * s2)
            if transpose_out:
                o_ref[0, c0 + l0:c0 + l0 + LANES, :] = rot.T.astype(o_ref.dtype)
            else:
                o_ref[:, c0 + l0:c0 + l0 + LANES] = rot.astype(o_ref.dtype)
    if transposed_extra:
        ot_ref[0] = lax.dot_general(wt_ref[...], hn, _NT, preferred_element_type=F32).astype(ot_ref.dtype)


def _norm_proj(h, g, w, *, tm, seg_gain=None, rope=None, n_rope_cols=0, seq_len=None, w_t=None,
               transpose_out=False):
    n, d = h.shape
    n_out = w.shape[1]
    in_specs = [pl.BlockSpec((tm, d), lambda i: (i, 0)), _resident((1, d)), _resident((d, n_out))]
    args = [h, g.reshape(1, d), w]
    if w_t is not None:
        in_specs.append(_resident(w_t.shape))
        args.append(w_t)
    if n_rope_cols:
        tiles_per_seq = seq_len // tm
        tab = pl.BlockSpec((tm, LANES), lambda i: (i % tiles_per_seq, 0))
        in_specs += [_resident((1, n_rope_cols)), tab, tab, tab]
        args += [seg_gain.reshape(1, n_rope_cols), *rope]
    if transpose_out:
        assert n_rope_cols == n_out
        out_specs = pl.BlockSpec((1, n_out, tm), lambda i: (i, 0, 0))
        out_shape = jax.ShapeDtypeStruct((n // tm, n_out, tm), BF16)
    else:
        out_specs = pl.BlockSpec((tm, n_out), lambda i: (i, 0))
        out_shape = jax.ShapeDtypeStruct((n, n_out), BF16)
    if w_t is not None:
        out_specs = [out_specs, pl.BlockSpec((1, w_t.shape[0], tm), lambda i: (i, 0, 0))]
        out_shape = [out_shape, jax.ShapeDtypeStruct((n // tm, w_t.shape[0], tm), BF16)]
    return pl.pallas_call(
        functools.partial(_norm_proj_kernel, n_rope_cols=n_rope_cols, transposed_extra=w_t is not None,
                          transpose_out=transpose_out),
        grid=(n // tm,),
        in_specs=in_specs,
        out_specs=out_specs,
        out_shape=out_shape,
        compiler_params=pltpu.CompilerParams(
            dimension_semantics=("parallel",), vmem_limit_bytes=VMEM_LIMIT),
        name="norm_proj_rope" if n_rope_cols else "norm_proj",
    )(*args)


def _stack_halves(q, lo_half):
    zero = jnp.zeros_like(q)
    return jnp.concatenate([jnp.where(lo_half, q, zero), jnp.where(lo_half, zero, q)], axis=0)


def _neg_abs(x):
    sign = jnp.uint32(0x80000000)
    return lax.bitcast_convert_type(lax.bitcast_convert_type(x, jnp.uint32) | sign, F32)


def _all_of(*masks):
    masks = [m for m in masks if m is not None]
    return functools.reduce(jnp.logical_and, masks) if masks else None


def _visit_blocks(n_steps, produce, consume, buf_a, buf_b, first, last, only):
    @pl.when(n_steps == 1)
    def _():
        produce(0, buf_a, only)
        consume(0, buf_a)

    @pl.when(n_steps > 1)
    def _():
        produce(0, buf_a, first)

        def two_steps(k, _):
            t = 2 * k
            produce(t + 1, buf_b, None)
            consume(t, buf_a)
            produce(t + 2, buf_a, None)
            consume(t + 1, buf_b)
            return 0

        lax.fori_loop(0, (n_steps - 2) // 2, two_steps, 0)

        @pl.when(n_steps % 2 == 0)
        def _():
            produce(n_steps - 1, buf_b, last)
            consume(n_steps - 2, buf_a)
            consume(n_steps - 1, buf_b)

        @pl.when(n_steps % 2 == 1)
        def _():
            produce(n_steps - 2, buf_b, None)
            consume(n_steps - 3, buf_a)
            produce(n_steps - 1, buf_a, last)
            consume(n_steps - 2, buf_b)
            consume(n_steps - 1, buf_a)


def _sb_attn_kernel(q_ref, k_ref, v_ref, o_ref, y_a, y_b, carry_ref, acc_ref, *, n_pairs, n_pad):
    qi = pl.program_id(2)
    blk = q_ref.shape[1]
    pairs = range(n_pairs)
    lo_half = lax.broadcasted_iota(jnp.int32, (1, LANES), 1) < HEAD_DIM
    qs = [_stack_halves(q_ref[0, :, p * LANES:(p + 1) * LANES], lo_half) for p in pairs]
    row = lax.broadcasted_iota(jnp.int32, (2 * blk, blk), 0)
    col = lax.broadcasted_iota(jnp.int32, (2 * blk, blk), 1)
    key_j = lax.broadcasted_iota(jnp.int32, (blk, blk), 0)
    key_s = lax.broadcasted_iota(jnp.int32, (blk, blk), 1)
    neg_tri = jnp.where(key_j >= key_s, -1.0, 0.0).astype(BF16)
    past = col < row % blk
    real_key = col >= n_pad if n_pad else None

    def logits(t, y_ref, mask):
        start = pl.multiple_of((qi - t) * blk, blk)
        for p in pairs:
            kb = k_ref[0, pl.ds(start, blk), p * LANES:(p + 1) * LANES]
            y = lax.dot_general(qs[p], kb, _NT, preferred_element_type=F32)
            y_ref[p] = y if mask is None else jnp.where(mask, y, -jnp.inf)

    def absorb(t, y_ref):
        start = pl.multiple_of((qi - t) * blk, blk)
        vb = [v_ref[0, pl.ds(start, blk), p * LANES:(p + 1) * LANES] for p in pairs]
        sps = []
        for p in pairs:
            y = y_ref[p]
            sp = jnp.maximum(y, 0.0) + jnp.log2(1.0 + jnp.exp2(_neg_abs(y)))
            sps.append(sp.astype(BF16))
        cum = [_dot(sps[p], neg_tri) for p in pairs]
        for p in pairs:
            carry = carry_ref[p]
            w = jnp.exp2(y_ref[p] + cum[p] + jnp.concatenate([carry, carry], axis=1)).astype(BF16)
            acc_ref[p, 0] += _dot(w[:blk], vb[p])
            acc_ref[p, 1] += _dot(w[blk:], vb[p])
            carry_ref[p] = carry + jnp.broadcast_to(cum[p][:, 0:1], carry.shape)

    carry_ref[...] = jnp.zeros_like(carry_ref)
    acc_ref[...] = jnp.zeros_like(acc_ref)
    _visit_blocks(qi + 1, logits, absorb, y_a, y_b, first=past, last=real_key, only=_all_of(past, real_key))
    for p in pairs:
        o_ref[0, :, p * LANES:(p + 1) * LANES] = jnp.where(
            lo_half, acc_ref[p, 0], acc_ref[p, 1]).astype(o_ref.dtype)


def _sb_attention(qkv, batch, seq_len, n_pad):
    d = SB_HEADS * HEAD_DIM
    width = SB_PAIRS_PER_STEP * LANES
    groups = d // width
    qkv3 = qkv.reshape(batch, seq_len, 3 * d)
    out = pl.pallas_call(
        functools.partial(_sb_attn_kernel, n_pairs=SB_PAIRS_PER_STEP, n_pad=n_pad),
        grid=(batch, groups, seq_len // ATTN_BLOCK),
        in_specs=[
            pl.BlockSpec((1, ATTN_BLOCK, width), lambda b, g, i: (b, i, g)),
            pl.BlockSpec((1, seq_len, width), lambda b, g, i: (b, 0, groups + g)),
            pl.BlockSpec((1, seq_len, width), lambda b, g, i: (b, 0, 2 * groups + g)),
        ],
        out_specs=pl.BlockSpec((1, ATTN_BLOCK, width), lambda b, g, i: (b, i, g)),
        out_shape=jax.ShapeDtypeStruct((batch, seq_len, d), BF16),
        scratch_shapes=[pltpu.VMEM((SB_PAIRS_PER_STEP, 2 * ATTN_BLOCK, ATTN_BLOCK), F32)] * 2 + [
            pltpu.VMEM((SB_PAIRS_PER_STEP, 2 * ATTN_BLOCK, LANES), F32),
            pltpu.VMEM((SB_PAIRS_PER_STEP, 2, ATTN_BLOCK, LANES), F32)],
        compiler_params=pltpu.CompilerParams(
            dimension_semantics=("parallel", "parallel", "arbitrary"), vmem_limit_bytes=VMEM_LIMIT),
        name="sb_attention",
    )(qkv3, qkv3, qkv3)
    return out.reshape(batch * seq_len, d)


def _diff_attn_kernel(lam_ref, qt_ref, k_ref, vt_ref, g_ref, o_ref, s_a, s_b, m_ref, l_ref, acc_ref, *,
                      n_heads, n_pad, lam_init):
    qi = pl.program_id(2)
    blk = qt_ref.shape[3]
    heads = range(n_heads)
    first_map = lax.broadcasted_iota(jnp.int32, (LANES, 1), 0) < HEAD_DIM
    qs = []
    for h in heads:
        qt = qt_ref[0, 0, h * LANES:(h + 1) * LANES, :]
        zero = jnp.zeros_like(qt)
        qs.append(jnp.concatenate([jnp.where(first_map, qt, zero), jnp.where(first_map, zero, qt)], axis=1))
    key = lax.broadcasted_iota(jnp.int32, (blk, 2 * blk), 0)
    qry = lax.broadcasted_iota(jnp.int32, (blk, 2 * blk), 1) % blk
    causal = key <= qry
    real_key = key >= n_pad if n_pad else None
    only = _all_of(causal, jnp.logical_or(real_key, qry < n_pad)) if n_pad else causal

    def scores(t, s_ref, mask):
        start = pl.multiple_of(t * blk, blk)
        for h in heads:
            kb = k_ref[0, pl.ds(start, blk), h * LANES:(h + 1) * LANES]
            sc = _dot(kb, qs[h])
            s_ref[h] = sc if mask is None else jnp.where(mask, sc, -jnp.inf)

    def absorb(t, s_ref):
        vt = [vt_ref[0, t, h * LANES:(h + 1) * LANES, :] for h in heads]
        p, alpha = [], []
        for h in heads:
            sh = s_ref[h]
            m = m_ref[h]
            m_new = jnp.maximum(m, jnp.max(sh, axis=0, keepdims=True))
            alpha.append(jnp.exp2(m - m_new))
            ph = jnp.exp2(sh - m_new)
            l_ref[h] = alpha[h] * l_ref[h] + jnp.sum(ph, axis=0, keepdims=True)
            m_ref[h] = m_new
            p.append(ph.astype(BF16))
        pv = [_dot(vt[h], p[h]) for h in heads]
        for h in heads:
            acc_ref[h] = alpha[h] * acc_ref[h] + pv[h]

    m_ref[...] = jnp.full_like(m_ref, -jnp.inf)
    l_ref[...] = jnp.zeros_like(l_ref)
    acc_ref[...] = jnp.zeros_like(acc_ref)
    _visit_blocks(qi + 1, scores, absorb, s_a, s_b, first=real_key, last=causal, only=only)

    lp = lam_ref[...]
    lam = (jnp.exp(jnp.sum(lp[0:1] * lp[1:2], axis=-1, keepdims=True))
           - jnp.exp(jnp.sum(lp[2:3] * lp[3:4], axis=-1, keepdims=True)) + lam_init)
    for h in heads:
        o = acc_ref[h] * (1.0 / l_ref[h])
        o = o[:, :blk] - lam * o[:, blk:]
        ms = jnp.mean(o * o, axis=0, keepdims=True)
        o = o * lax.rsqrt(ms + EPS) * (g_ref[...] * (1.0 - lam_init))
        o_ref[0, :, h * LANES:(h + 1) * LANES] = o.T.astype(o_ref.dtype)


def _diff_attention(qt, k, vt, lam_params, subln_g, lam_init, batch, seq_len, n_pad):
    d = DIFF_HEADS * LANES
    width = DIFF_HEADS_PER_STEP * LANES
    groups = d // width
    n_blk = seq_len // ATTN_BLOCK
    qt4 = qt.reshape(batch, n_blk, d, ATTN_BLOCK)
    k3 = k.reshape(batch, seq_len, d)
    vt4 = vt.reshape(batch, n_blk, d, ATTN_BLOCK)
    out = pl.pallas_call(
        functools.partial(_diff_attn_kernel, n_heads=DIFF_HEADS_PER_STEP, n_pad=n_pad, lam_init=lam_init),
        grid=(batch, groups, seq_len // ATTN_BLOCK),
        in_specs=[
            pl.BlockSpec(lam_params.shape, lambda b, g, i: (0, 0)),
            pl.BlockSpec((1, 1, width, ATTN_BLOCK), lambda b, g, i: (b, i, g, 0)),
            pl.BlockSpec((1, seq_len, width), lambda b, g, i: (b, 0, g)),
            pl.BlockSpec((1, n_blk, width, ATTN_BLOCK), lambda b, g, i: (b, 0, g, 0)),
            pl.BlockSpec((LANES, 1), lambda b, g, i: (0, 0)),
        ],
        out_specs=pl.BlockSpec((1, ATTN_BLOCK, width), lambda b, g, i: (b, i, g)),
        out_shape=jax.ShapeDtypeStruct((batch, seq_len, d), BF16),
        scratch_shapes=[pltpu.VMEM((DIFF_HEADS_PER_STEP, ATTN_BLOCK, 2 * ATTN_BLOCK), F32)] * 2 + [
            pltpu.VMEM((DIFF_HEADS_PER_STEP, 1, 2 * ATTN_BLOCK), F32)] * 2 + [
            pltpu.VMEM((DIFF_HEADS_PER_STEP, LANES, 2 * ATTN_BLOCK), F32)],
        compiler_params=pltpu.CompilerParams(
            dimension_semantics=("parallel", "parallel", "arbitrary"), vmem_limit_bytes=VMEM_LIMIT),
        name="diff_attention",
    )(lam_params, qt4, k3, vt4, subln_g.reshape(LANES, 1))
    return out.reshape(batch * seq_len, d)


def _post_attn_kernel(h_ref, o_ref, wo_ref, g_ref, win_ref, wout_ref, out_ref):
    h1 = h_ref[...] + _dot(o_ref[...], wo_ref[...])
    hn = (h1 * _rms_scale(h1) * g_ref[...]).astype(BF16)
    acc = h1
    d_ff = win_ref.shape[1]
    for c0 in range(0, d_ff, FF_CHUNK):
        a = jnp.maximum(_dot(hn, win_ref[:, c0:c0 + FF_CHUNK]), 0.0)
        acc = acc + _dot((a * a).astype(BF16), wout_ref[c0:c0 + FF_CHUNK, :])
    out_ref[...] = acc


def _post_attention(h, o, wo, g, w_in, w_out, *, tm):
    n, d = h.shape
    row = pl.BlockSpec((tm, d), lambda i: (i, 0))
    return pl.pallas_call(
        _post_attn_kernel,
        grid=(n // tm,),
        in_specs=[row, pl.BlockSpec((tm, o.shape[1]), lambda i: (i, 0)), _resident(wo.shape),
                  _resident((1, d)), _resident(w_in.shape), _resident(w_out.shape)],
        out_specs=row,
        out_shape=jax.ShapeDtypeStruct((n, d), F32),
        compiler_params=pltpu.CompilerParams(
            dimension_semantics=("parallel",), vmem_limit_bytes=VMEM_LIMIT),
        name="post_attention",
    )(h, o, wo, g.reshape(1, d), w_in, w_out)


def _rope_tables(seq_len, n_pad):
    half = ROPE_DIM // 2
    inv_freq = jnp.power(ROPE_THETA, -jnp.arange(0, ROPE_DIM, 2, dtype=F32) / ROPE_DIM)
    pos = jnp.maximum(jnp.arange(seq_len) - n_pad, 0).astype(F32)
    ang = pos[:, None] * inv_freq[None, :]
    cos, sin = jnp.cos(ang), jnp.sin(ang)
    seg_lane = jnp.arange(LANES) % HEAD_DIM
    idx = seg_lane % half
    cos_t = jnp.where(seg_lane < ROPE_DIM, cos[:, idx], 1.0)
    s1_t = jnp.where(seg_lane < half, -sin[:, idx], 0.0)
    s2_t = jnp.where((seg_lane >= half) & (seg_lane < ROPE_DIM), sin[:, idx], 0.0)
    return cos_t, s1_t, s2_t


def kernel(x, meta_tokens, a_norm_g, a_w_qkv, a_w_o, kv_norm_g, kv_w, kv_k_norm_g, b_norm_g, b_w_q,
           b_q_norm_g, b_lambda, b_subln_g, b_w_o, mlp_norm_g, mlp_w_in, mlp_w_out):
    batch, s_len, d = x.shape
    n_a = a_w_qkv.shape[0]
    n_b = b_w_q.shape[0]
    real = s_len + N_META
    seq_len = -(-real // ATTN_BLOCK) * ATTN_BLOCK
    n_pad = seq_len - real
    n_rows = batch * seq_len
    tm = ROW_TILE if n_rows % ROW_TILE == 0 else ATTN_BLOCK

    meta = jnp.broadcast_to(meta_tokens.astype(x.dtype)[None], (batch, N_META, d))
    h = jnp.concatenate([jnp.zeros((batch, n_pad, d), x.dtype), meta, x], axis=1)
    h = h.reshape(n_rows, d)
    rope = _rope_tables(seq_len, n_pad)
    k_cols = DIFF_HEADS * 2 * HEAD_DIM
    qk_scale = HEAD_DIM ** -0.5 * math.log2(math.e)

    kv = None
    for layer in range(n_a + n_b):
        if layer < n_a:
            i = layer
            col_scale = jnp.concatenate([jnp.full((d,), qk_scale, F32), jnp.ones((2 * d,), F32)])
            w_qkv = (a_w_qkv[i] * col_scale).astype(BF16)
            qkv = _norm_proj(h, a_norm_g[i], w_qkv, tm=tm)
            o = _sb_attention(qkv, batch, seq_len, n_pad)
            w_o = a_w_o[i]
        else:
            j = layer - n_a
            if kv is None:
                k_gain = jnp.tile(kv_k_norm_g.reshape(2 * HEAD_DIM), DIFF_HEADS)
                kv = _norm_proj(h, kv_norm_g, kv_w[:, :k_cols].astype(BF16), tm=ATTN_BLOCK, seg_gain=k_gain,
                                rope=rope, n_rope_cols=k_cols, seq_len=seq_len,
                                w_t=kv_w[:, k_cols:].T.astype(BF16))
            lam_init = 0.8 - 0.6 * math.exp(-0.3 * layer)
            q_gain = jnp.tile(b_q_norm_g[j].reshape(2 * HEAD_DIM), DIFF_HEADS) * qk_scale
            q = _norm_proj(h, b_norm_g[j], b_w_q[j].astype(BF16), tm=ATTN_BLOCK, seg_gain=q_gain,
                           rope=rope, n_rope_cols=k_cols, seq_len=seq_len, transpose_out=True)
            o = _diff_attention(q, kv[0], kv[1], b_lambda[j].astype(F32), b_subln_g[j], lam_init, batch,
                                seq_len, n_pad)
            w_o = b_w_o[j]
        h = _post_attention(h, o, w_o.astype(BF16), mlp_norm_g[layer], mlp_w_in[layer].astype(BF16),
                            mlp_w_out[layer].astype(BF16), tm=tm)

    return h.reshape(batch, seq_len, d)[:, n_pad + N_META:]
```

```python
import functools
import math

import jax
import jax.numpy as jnp
from jax import lax
from jax.experimental import pallas as pl
from jax.experimental.pallas import tpu as pltpu

N_META = 16
SB_HEADS = 16
HEAD_DIM = 64
DIFF_HEADS = 8
ROPE_DIM = 16
ROPE_THETA = 500000.0
EPS = 1e-6

LANES = 128
ATTN_BLOCK = 256
ROW_TILE = 512
FF_CHUNK = 1024
SB_PAIRS_PER_STEP = 4
DIFF_HEADS_PER_STEP = 4
VMEM_LIMIT = 56 * 1024 * 1024

F32 = jnp.float32
BF16 = jnp.bfloat16
_NT = (((1,), (1,)), ((), ()))


def _dot(a, b):
    return jnp.dot(a, b, preferred_element_type=F32)


def _rms_scale(x):
    return lax.rsqrt(jnp.mean(x * x, axis=-1, keepdims=True) + EPS)


def _resident(shape):
    nd = len(shape)
    return pl.BlockSpec(shape, lambda *_: (0,) * nd, pipeline_mode=pl.Buffered(1))


def _norm_proj_kernel(h_ref, g_ref, w_ref, *rest, n_rope_cols, transposed_extra):
    wt_ref = ot_ref = None
    if transposed_extra:
        wt_ref, *rest, ot_ref = rest
    if n_rope_cols:
        sg_ref, cos_ref, s1_ref, s2_ref, o_ref = rest
    else:
        (o_ref,) = rest
    x = h_ref[...]
    hn = (x * _rms_scale(x) * g_ref[...]).astype(BF16)
    n_out = o_ref.shape[1]
    if n_rope_cols:
        r = lax.broadcasted_iota(jnp.int32, (LANES, LANES), 0) // HEAD_DIM
        c = lax.broadcasted_iota(jnp.int32, (LANES, LANES), 1) // HEAD_DIM
        seg = jnp.where(r == c, 1.0, 0.0).astype(BF16)
        cos, s1, s2 = cos_ref[...], s1_ref[...], s2_ref[...]
    chunk = min(FF_CHUNK, n_out)
    for c0 in range(0, n_out, chunk):
        y = _dot(hn, w_ref[:, c0:c0 + chunk])
        if c0 >= n_rope_cols:
            o_ref[:, c0:c0 + chunk] = y.astype(o_ref.dtype)
            continue
        for l0 in range(0, chunk, LANES):
            yc = y[:, l0:l0 + LANES]
            ms = _dot((yc * yc).astype(BF16), seg) * (1.0 / HEAD_DIM)
            yn = yc * lax.rsqrt(ms + EPS) * sg_ref[:, c0 + l0:c0 + l0 + LANES]
            rot = (yn * cos
                   + pltpu.roll(yn, LANES - ROPE_DIM // 2, 1) * s1
                   + pltpu.roll(yn, ROPE_DIM // 2, 1) * s2)
            o_ref[:, c0 + l0:c0 + l0 + LANES] = rot.astype(o_ref.dtype)
    if transposed_extra:
        ot_ref[0] = lax.dot_general(wt_ref[...], hn, _NT, preferred_element_type=F32).astype(ot_ref.dtype)


def _norm_proj(h, g, w, *, tm, seg_gain=None, rope=None, n_rope_cols=0, seq_len=None, w_t=None):
    n, d = h.shape
    n_out = w.shape[1]
    in_specs = [pl.BlockSpec((tm, d), lambda i: (i, 0)), _resident((1, d)), _resident((d, n_out))]
    args = [h, g.reshape(1, d), w]
    if w_t is not None:
        in_specs.append(_resident(w_t.shape))
        args.append(w_t)
    if n_rope_cols:
        tiles_per_seq = seq_len // tm
        tab = pl.BlockSpec((tm, LANES), lambda i: (i % tiles_per_seq, 0))
        in_specs += [_resident((1, n_rope_cols)), tab, tab, tab]
        args += [seg_gain.reshape(1, n_rope_cols), *rope]
    out_specs = pl.BlockSpec((tm, n_out), lambda i: (i, 0))
    out_shape = jax.ShapeDtypeStruct((n, n_out), BF16)
    if w_t is not None:
        out_specs = [out_specs, pl.BlockSpec((1, w_t.shape[0], tm), lambda i: (i, 0, 0))]
        out_shape = [out_shape, jax.ShapeDtypeStruct((n // tm, w_t.shape[0], tm), BF16)]
    return pl.pallas_call(
        functools.partial(_norm_proj_kernel, n_rope_cols=n_rope_cols, transposed_extra=w_t is not None),
        grid=(n // tm,),
        in_specs=in_specs,
        out_specs=out_specs,
        out_shape=out_shape,
        compiler_params=pltpu.CompilerParams(
            dimension_semantics=("parallel",), vmem_limit_bytes=VMEM_LIMIT),
        name="norm_proj_rope" if n_rope_cols else "norm_proj",
    )(*args)


def _stack_halves(q, lo_half):
    zero = jnp.zeros_like(q)
    return jnp.concatenate([jnp.where(lo_half, q, zero), jnp.where(lo_half, zero, q)], axis=0)


def _neg_abs(x):
    sign = jnp.uint32(0x80000000)
    return lax.bitcast_convert_type(lax.bitcast_convert_type(x, jnp.uint32) | sign, F32)


def _all_of(*masks):
    masks = [m for m in masks if m is not None]
    return functools.reduce(jnp.logical_and, masks) if masks else None


def _visit_blocks(n_steps, produce, consume, produce_and_consume, buf_a, buf_b, first, last, only):
    @pl.when(n_steps == 1)
    def _():
        produce(0, buf_a, only)
        consume(0, buf_a)

    @pl.when(n_steps > 1)
    def _():
        produce(0, buf_a, first)

        def two_steps(k, _):
            t = 2 * k
            produce_and_consume(t + 1, buf_b, None, t, buf_a)
            produce_and_consume(t + 2, buf_a, None, t + 1, buf_b)
            return 0

        lax.fori_loop(0, (n_steps - 2) // 2, two_steps, 0)

        @pl.when(n_steps % 2 == 0)
        def _():
            produce_and_consume(n_steps - 1, buf_b, last, n_steps - 2, buf_a)
            consume(n_steps - 1, buf_b)

        @pl.when(n_steps % 2 == 1)
        def _():
            produce_and_consume(n_steps - 2, buf_b, None, n_steps - 3, buf_a)
            produce_and_consume(n_steps - 1, buf_a, last, n_steps - 2, buf_b)
            consume(n_steps - 1, buf_a)


def _sb_attn_kernel(q_ref, k_ref, v_ref, o_ref, y_a, y_b, carry_ref, acc_ref, *, n_pairs, n_pad):
    qi = pl.program_id(2)
    blk = q_ref.shape[1]
    pairs = range(n_pairs)
    lo_half = lax.broadcasted_iota(jnp.int32, (1, LANES), 1) < HEAD_DIM
    qs = [_stack_halves(q_ref[0, :, p * LANES:(p + 1) * LANES], lo_half) for p in pairs]
    row = lax.broadcasted_iota(jnp.int32, (2 * blk, blk), 0)
    col = lax.broadcasted_iota(jnp.int32, (2 * blk, blk), 1)
    key_j = lax.broadcasted_iota(jnp.int32, (blk, blk), 0)
    key_s = lax.broadcasted_iota(jnp.int32, (blk, blk), 1)
    neg_tri = jnp.where(key_j >= key_s, -1.0, 0.0).astype(BF16)
    past = col < row % blk
    real_key = col >= n_pad if n_pad else None

    def logits_pair(t, y_ref, mask, p):
        start = pl.multiple_of((qi - t) * blk, blk)
        kb = k_ref[0, pl.ds(start, blk), p * LANES:(p + 1) * LANES]
        y = lax.dot_general(qs[p], kb, _NT, preferred_element_type=F32)
        y_ref[p] = y if mask is None else jnp.where(mask, y, -jnp.inf)

    def logits(t, y_ref, mask):
        for p in pairs:
            logits_pair(t, y_ref, mask, p)

    def cum_pair(y_ref, p):
        y = y_ref[p]
        sp = jnp.maximum(y, 0.0) + jnp.log2(1.0 + jnp.exp2(_neg_abs(y)))
        return _dot(sp.astype(BF16), neg_tri)

    def weigh_pair(t, y_ref, p, cum):
        start = pl.multiple_of((qi - t) * blk, blk)
        vb = v_ref[0, pl.ds(start, blk), p * LANES:(p + 1) * LANES]
        carry = carry_ref[p]
        w = jnp.exp2(y_ref[p] + cum + jnp.concatenate([carry, carry], axis=1)).astype(BF16)
        acc_ref[p, 0] += _dot(w[:blk], vb)
        acc_ref[p, 1] += _dot(w[blk:], vb)
        carry_ref[p] = carry + jnp.broadcast_to(cum[:, 0:1], carry.shape)

    def absorb(t, y_ref):
        cum = [cum_pair(y_ref, p) for p in pairs]
        for p in pairs:
            weigh_pair(t, y_ref, p, cum[p])

    def logits_and_absorb(t_next, next_ref, mask, t, y_ref):
        cum = {}
        for p in pairs:
            logits_pair(t_next, next_ref, mask, p)
            cum[p] = cum_pair(y_ref, p)
            if p > 0:
                weigh_pair(t, y_ref, p - 1, cum[p - 1])
        weigh_pair(t, y_ref, n_pairs - 1, cum[n_pairs - 1])

    carry_ref[...] = jnp.zeros_like(carry_ref)
    acc_ref[...] = jnp.zeros_like(acc_ref)
    _visit_blocks(qi + 1, logits, absorb, logits_and_absorb, y_a, y_b,
                  first=past, last=real_key, only=_all_of(past, real_key))
    for p in pairs:
        o_ref[0, :, p * LANES:(p + 1) * LANES] = jnp.where(
            lo_half, acc_ref[p, 0], acc_ref[p, 1]).astype(o_ref.dtype)


def _sb_attention(qkv, batch, seq_len, n_pad):
    d = SB_HEADS * HEAD_DIM
    width = SB_PAIRS_PER_STEP * LANES
    groups = d // width
    qkv3 = qkv.reshape(batch, seq_len, 3 * d)
    out = pl.pallas_call(
        functools.partial(_sb_attn_kernel, n_pairs=SB_PAIRS_PER_STEP, n_pad=n_pad),
        grid=(batch, groups, seq_len // ATTN_BLOCK),
        in_specs=[
            pl.BlockSpec((1, ATTN_BLOCK, width), lambda b, g, i: (b, i, g)),
            pl.BlockSpec((1, seq_len, width), lambda b, g, i: (b, 0, groups + g)),
            pl.BlockSpec((1, seq_len, width), lambda b, g, i: (b, 0, 2 * groups + g)),
        ],
        out_specs=pl.BlockSpec((1, ATTN_BLOCK, width), lambda b, g, i: (b, i, g)),
        out_shape=jax.ShapeDtypeStruct((batch, seq_len, d), BF16),
        scratch_shapes=[pltpu.VMEM((SB_PAIRS_PER_STEP, 2 * ATTN_BLOCK, ATTN_BLOCK), F32)] * 2 + [
            pltpu.VMEM((SB_PAIRS_PER_STEP, 2 * ATTN_BLOCK, LANES), F32),
            pltpu.VMEM((SB_PAIRS_PER_STEP, 2, ATTN_BLOCK, LANES), F32)],
        compiler_params=pltpu.CompilerParams(
            dimension_semantics=("parallel", "parallel", "arbitrary"), vmem_limit_bytes=VMEM_LIMIT),
        name="sb_attention",
    )(qkv3, qkv3, qkv3)
    return out.reshape(batch * seq_len, d)


def _diff_attn_kernel(lam_ref, q_ref, k_ref, vt_ref, g_ref, o_ref, s_a, s_b, m_ref, l_ref, acc_ref, *,
                      n_heads, n_pad, lam_init):
    qi = pl.program_id(2)
    blk = q_ref.shape[1]
    heads = range(n_heads)
    lo_half = lax.broadcasted_iota(jnp.int32, (1, LANES), 1) < HEAD_DIM
    qs = [_stack_halves(q_ref[0, :, h * LANES:(h + 1) * LANES], lo_half) for h in heads]
    key = lax.broadcasted_iota(jnp.int32, (blk, 2 * blk), 0)
    qry = lax.broadcasted_iota(jnp.int32, (blk, 2 * blk), 1) % blk
    causal = key <= qry
    real_key = key >= n_pad if n_pad else None
    only = _all_of(causal, jnp.logical_or(real_key, qry < n_pad)) if n_pad else causal

    def scores_head(t, s_ref, mask, h):
        start = pl.multiple_of(t * blk, blk)
        kb = k_ref[0, pl.ds(start, blk), h * LANES:(h + 1) * LANES]
        sc = lax.dot_general(kb, qs[h], _NT, preferred_element_type=F32)
        s_ref[h] = sc if mask is None else jnp.where(mask, sc, -jnp.inf)

    def scores(t, s_ref, mask):
        for h in heads:
            scores_head(t, s_ref, mask, h)

    def softmax_head(s_ref, h):
        sh = s_ref[h]
        m = m_ref[h]
        m_new = jnp.maximum(m, jnp.max(sh, axis=0, keepdims=True))
        alpha = jnp.exp2(m - m_new)
        ph = jnp.exp2(sh - m_new)
        l_ref[h] = alpha * l_ref[h] + jnp.sum(ph, axis=0, keepdims=True)
        m_ref[h] = m_new
        return alpha, ph.astype(BF16)

    def accumulate(t, h, alpha, ph):
        acc_ref[h] = alpha * acc_ref[h] + _dot(vt_ref[0, t, h * LANES:(h + 1) * LANES, :], ph)

    def absorb(t, s_ref):
        for h in heads:
            accumulate(t, h, *softmax_head(s_ref, h))

    def scores_and_absorb(t_next, next_ref, mask, t, s_ref):
        scores_head(t_next, next_ref, mask, 0)
        for h in heads:
            alpha, ph = softmax_head(s_ref, h)
            if h + 1 < n_heads:
                scores_head(t_next, next_ref, mask, h + 1)
            accumulate(t, h, alpha, ph)

    m_ref[...] = jnp.full_like(m_ref, -jnp.inf)
    l_ref[...] = jnp.zeros_like(l_ref)
    acc_ref[...] = jnp.zeros_like(acc_ref)
    _visit_blocks(qi + 1, scores, absorb, scores_and_absorb, s_a, s_b, first=real_key, last=causal, only=only)

    lp = lam_ref[...]
    lam = (jnp.exp(jnp.sum(lp[0:1] * lp[1:2], axis=-1, keepdims=True))
           - jnp.exp(jnp.sum(lp[2:3] * lp[3:4], axis=-1, keepdims=True)) + lam_init)
    for h in heads:
        o = acc_ref[h] * (1.0 / l_ref[h])
        o = o[:, :blk] - lam * o[:, blk:]
        ms = jnp.mean(o * o, axis=0, keepdims=True)
        o = o * lax.rsqrt(ms + EPS) * (g_ref[...] * (1.0 - lam_init))
        o_ref[0, :, h * LANES:(h + 1) * LANES] = o.T.astype(o_ref.dtype)


def _diff_attention(q, k, vt, lam_params, subln_g, lam_init, batch, seq_len, n_pad):
    d = DIFF_HEADS * LANES
    width = DIFF_HEADS_PER_STEP * LANES
    groups = d // width
    n_blk = seq_len // ATTN_BLOCK
    q3 = q.reshape(batch, seq_len, d)
    k3 = k.reshape(batch, seq_len, d)
    vt4 = vt.reshape(batch, n_blk, d, ATTN_BLOCK)
    out = pl.pallas_call(
        functools.partial(_diff_attn_kernel, n_heads=DIFF_HEADS_PER_STEP, n_pad=n_pad, lam_init=lam_init),
        grid=(batch, groups, seq_len // ATTN_BLOCK),
        in_specs=[
            pl.BlockSpec(lam_params.shape, lambda b, g, i: (0, 0)),
            pl.BlockSpec((1, ATTN_BLOCK, width), lambda b, g, i: (b, i, g)),
            pl.BlockSpec((1, seq_len, width), lambda b, g, i: (b, 0, g)),
            pl.BlockSpec((1, n_blk, width, ATTN_BLOCK), lambda b, g, i: (b, 0, g, 0)),
            pl.BlockSpec((LANES, 1), lambda b, g, i: (0, 0)),
        ],
        out_specs=pl.BlockSpec((1, ATTN_BLOCK, width), lambda b, g, i: (b, i, g)),
        out_shape=jax.ShapeDtypeStruct((batch, seq_len, d), BF16),
        scratch_shapes=[pltpu.VMEM((DIFF_HEADS_PER_STEP, ATTN_BLOCK, 2 * ATTN_BLOCK), F32)] * 2 + [
            pltpu.VMEM((DIFF_HEADS_PER_STEP, 1, 2 * ATTN_BLOCK), F32)] * 2 + [
            pltpu.VMEM((DIFF_HEADS_PER_STEP, LANES, 2 * ATTN_BLOCK), F32)],
        compiler_params=pltpu.CompilerParams(
            dimension_semantics=("parallel", "parallel", "arbitrary"), vmem_limit_bytes=VMEM_LIMIT),
        name="diff_attention",
    )(lam_params, q3, k3, vt4, subln_g.reshape(LANES, 1))
    return out.reshape(batch * seq_len, d)


def _post_attn_kernel(h_ref, o_ref, wo_ref, g_ref, win_ref, wout_ref, out_ref):
    h1 = h_ref[...] + _dot(o_ref[...], wo_ref[...])
    hn = (h1 * _rms_scale(h1) * g_ref[...]).astype(BF16)
    acc = h1
    d_ff = win_ref.shape[1]
    for c0 in range(0, d_ff, FF_CHUNK):
        a = jnp.maximum(_dot(hn, win_ref[:, c0:c0 + FF_CHUNK]), 0.0)
        acc = acc + _dot((a * a).astype(BF16), wout_ref[c0:c0 + FF_CHUNK, :])
    out_ref[...] = acc


def _post_attention(h, o, wo, g, w_in, w_out, *, tm):
    n, d = h.shape
    row = pl.BlockSpec((tm, d), lambda i: (i, 0))
    return pl.pallas_call(
        _post_attn_kernel,
        grid=(n // tm,),
        in_specs=[row, pl.BlockSpec((tm, o.shape[1]), lambda i: (i, 0)), _resident(wo.shape),
                  _resident((1, d)), _resident(w_in.shape), _resident(w_out.shape)],
        out_specs=row,
        out_shape=jax.ShapeDtypeStruct((n, d), F32),
        compiler_params=pltpu.CompilerParams(
            dimension_semantics=("parallel",), vmem_limit_bytes=VMEM_LIMIT),
        name="post_attention",
    )(h, o, wo, g.reshape(1, d), w_in, w_out)


def _rope_tables(seq_len, n_pad):
    half = ROPE_DIM // 2
    inv_freq = jnp.power(ROPE_THETA, -jnp.arange(0, ROPE_DIM, 2, dtype=F32) / ROPE_DIM)
    pos = jnp.maximum(jnp.arange(seq_len) - n_pad, 0).astype(F32)
    ang = pos[:, None] * inv_freq[None, :]
    cos, sin = jnp.cos(ang), jnp.sin(ang)
    seg_lane = jnp.arange(LANES) % HEAD_DIM
    idx = seg_lane % half
    cos_t = jnp.where(seg_lane < ROPE_DIM, cos[:, idx], 1.0)
    s1_t = jnp.where(seg_lane < half, -sin[:, idx], 0.0)
    s2_t = jnp.where((seg_lane >= half) & (seg_lane < ROPE_DIM), sin[:, idx], 0.0)
    return cos_t, s1_t, s2_t


def kernel(x, meta_tokens, a_norm_g, a_w_qkv, a_w_o, kv_norm_g, kv_w, kv_k_norm_g, b_norm_g, b_w_q,
           b_q_norm_g, b_lambda, b_subln_g, b_w_o, mlp_norm_g, mlp_w_in, mlp_w_out):
    batch, s_len, d = x.shape
    n_a = a_w_qkv.shape[0]
    n_b = b_w_q.shape[0]
    real = s_len + N_META
    seq_len = -(-real // ATTN_BLOCK) * ATTN_BLOCK
    n_pad = seq_len - real
    n_rows = batch * seq_len
    tm = ROW_TILE if n_rows % ROW_TILE == 0 else ATTN_BLOCK

    meta = jnp.broadcast_to(meta_tokens.astype(x.dtype)[None], (batch, N_META, d))
    h = jnp.concatenate([jnp.zeros((batch, n_pad, d), x.dtype), meta, x], axis=1)
    h = h.reshape(n_rows, d)
    rope = _rope_tables(seq_len, n_pad)
    k_cols = DIFF_HEADS * 2 * HEAD_DIM
    qk_scale = HEAD_DIM ** -0.5 * math.log2(math.e)

    kv = None
    for layer in range(n_a + n_b):
        if layer < n_a:
            i = layer
            col_scale = jnp.concatenate([jnp.full((d,), qk_scale, F32), jnp.ones((2 * d,), F32)])
            w_qkv = (a_w_qkv[i] * col_scale).astype(BF16)
            qkv = _norm_proj(h, a_norm_g[i], w_qkv, tm=tm)
            o = _sb_attention(qkv, batch, seq_len, n_pad)
            w_o = a_w_o[i]
        else:
            j = layer - n_a
            if kv is None:
                k_gain = jnp.tile(kv_k_norm_g.reshape(2 * HEAD_DIM), DIFF_HEADS)
                kv = _norm_proj(h, kv_norm_g, kv_w[:, :k_cols].astype(BF16), tm=ATTN_BLOCK, seg_gain=k_gain,
                                rope=rope, n_rope_cols=k_cols, seq_len=seq_len,
                                w_t=kv_w[:, k_cols:].T.astype(BF16))
            lam_init = 0.8 - 0.6 * math.exp(-0.3 * layer)
            q_gain = jnp.tile(b_q_norm_g[j].reshape(2 * HEAD_DIM), DIFF_HEADS) * qk_scale
            q = _norm_proj(h, b_norm_g[j], b_w_q[j].astype(BF16), tm=ATTN_BLOCK, seg_gain=q_gain,
                           rope=rope, n_rope_cols=k_cols, seq_len=seq_len)
            o = _diff_attention(q, kv[0], kv[1], b_lambda[j].astype(F32), b_subln_g[j], lam_init, batch,
                                seq_len, n_pad)
            w_o = b_w_o[j]
        h = _post_attention(h, o, w_o.astype(BF16), mlp_norm_g[layer], mlp_w_in[layer].astype(BF16),
                            mlp_w_out[layer].astype(BF16), tm=tm)

    return h.reshape(batch, seq_len, d)[:, n_pad + N_META:]
```

```python
import functools
import math

import jax
import jax.numpy as jnp
from jax import lax
from jax.experimental import pallas as pl
from jax.experimental.pallas import tpu as pltpu

N_META = 16
SB_HEADS = 16
HEAD_DIM = 64
DIFF_HEADS = 8
ROPE_DIM = 16
ROPE_THETA = 500000.0
EPS = 1e-6

LANES = 128
ATTN_BLOCK = 256
ROW_TILE = 512
FF_CHUNK = 1024
SB_PAIRS_PER_STEP = 4
DIFF_HEADS_PER_STEP = 4
VMEM_LIMIT = 56 * 1024 * 1024

F32 = jnp.float32
BF16 = jnp.bfloat16
_NT = (((1,), (1,)), ((), ()))


def _dot(a, b):
    return jnp.dot(a, b, preferred_element_type=F32)


def _rms_scale(x):
    return lax.rsqrt(jnp.mean(x * x, axis=-1, keepdims=True) + EPS)


def _resident(shape):
    nd = len(shape)
    return pl.BlockSpec(shape, lambda *_: (0,) * nd, pipeline_mode=pl.Buffered(1))


def _norm_proj_kernel(h_ref, g_ref, w_ref, *rest, n_rope_cols, transposed_extra):
    wt_ref = ot_ref = None
    if transposed_extra:
        wt_ref, *rest, ot_ref = rest
    if n_rope_cols:
        sg_ref, cos_ref, s1_ref, s2_ref, o_ref = rest
    else:
        (o_ref,) = rest
    x = h_ref[...]
    hn = (x * _rms_scale(x) * g_ref[...]).astype(BF16)
    n_out = o_ref.shape[1]
    if n_rope_cols:
        r = lax.broadcasted_iota(jnp.int32, (LANES, LANES), 0) // HEAD_DIM
        c = lax.broadcasted_iota(jnp.int32, (LANES, LANES), 1) // HEAD_DIM
        seg = jnp.where(r == c, 1.0, 0.0).astype(BF16)
        cos, s1, s2 = cos_ref[...], s1_ref[...], s2_ref[...]
    chunk = min(FF_CHUNK, n_out)
    for c0 in range(0, n_out, chunk):
        y = _dot(hn, w_ref[:, c0:c0 + chunk])
        if c0 >= n_rope_cols:
            o_ref[:, c0:c0 + chunk] = y.astype(o_ref.dtype)
            continue
        for l0 in range(0, chunk, LANES):
            yc = y[:, l0:l0 + LANES]
            ms = _dot((yc * yc).astype(BF16), seg) * (1.0 / HEAD_DIM)
            yn = yc * lax.rsqrt(ms + EPS) * sg_ref[:, c0 + l0:c0 + l0 + LANES]
            rot = (yn * cos
                   + pltpu.roll(yn, LANES - ROPE_DIM // 2, 1) * s1
                   + pltpu.roll(yn, ROPE_DIM // 2, 1) * s2)
            o_ref[:, c0 + l0:c0 + l0 + LANES] = rot.astype(o_ref.dtype)
    if transposed_extra:
        ot_ref[0] = lax.dot_general(wt_ref[...], hn, _NT, preferred_element_type=F32).astype(ot_ref.dtype)


def _norm_proj(h, g, w, *, tm, seg_gain=None, rope=None, n_rope_cols=0, seq_len=None, w_t=None):
    n, d = h.shape
    n_out = w.shape[1]
    in_specs = [pl.BlockSpec((tm, d), lambda i: (i, 0)), _resident((1, d)), _resident((d, n_out))]
    args = [h, g.reshape(1, d), w]
    if w_t is not None:
        in_specs.append(_resident(w_t.shape))
        args.append(w_t)
    if n_rope_cols:
        tiles_per_seq = seq_len // tm
        tab = pl.BlockSpec((tm, LANES), lambda i: (i % tiles_per_seq, 0))
        in_specs += [_resident((1, n_rope_cols)), tab, tab, tab]
        args += [seg_gain.reshape(1, n_rope_cols), *rope]
    out_specs = pl.BlockSpec((tm, n_out), lambda i: (i, 0))
    out_shape = jax.ShapeDtypeStruct((n, n_out), BF16)
    if w_t is not None:
        out_specs = [out_specs, pl.BlockSpec((1, w_t.shape[0], tm), lambda i: (i, 0, 0))]
        out_shape = [out_shape, jax.ShapeDtypeStruct((n // tm, w_t.shape[0], tm), BF16)]
    return pl.pallas_call(
        functools.partial(_norm_proj_kernel, n_rope_cols=n_rope_cols, transposed_extra=w_t is not None),
        grid=(n // tm,),
        in_specs=in_specs,
        out_specs=out_specs,
        out_shape=out_shape,
        compiler_params=pltpu.CompilerParams(
            dimension_semantics=("parallel",), vmem_limit_bytes=VMEM_LIMIT),
        name="norm_proj_rope" if n_rope_cols else "norm_proj",
    )(*args)


def _stack_halves(q, lo_half):
    zero = jnp.zeros_like(q)
    return jnp.concatenate([jnp.where(lo_half, q, zero), jnp.where(lo_half, zero, q)], axis=0)


def _neg_abs(x):
    sign = jnp.uint32(0x80000000)
    return lax.bitcast_convert_type(lax.bitcast_convert_type(x, jnp.uint32) | sign, F32)


def _all_of(*masks):
    masks = [m for m in masks if m is not None]
    return functools.reduce(jnp.logical_and, masks) if masks else None


def _visit_blocks(n_steps, produce, consume, produce_and_consume, buf_a, buf_b, first, last, only):
    @pl.when(n_steps == 1)
    def _():
        produce(0, buf_a, only)
        consume(0, buf_a)

    @pl.when(n_steps > 1)
    def _():
        produce(0, buf_a, first)

        def two_steps(k, _):
            t = 2 * k
            produce_and_consume(t + 1, buf_b, None, t, buf_a)
            produce_and_consume(t + 2, buf_a, None, t + 1, buf_b)
            return 0

        lax.fori_loop(0, (n_steps - 2) // 2, two_steps, 0)

        @pl.when(n_steps % 2 == 0)
        def _():
            produce_and_consume(n_steps - 1, buf_b, last, n_steps - 2, buf_a)
            consume(n_steps - 1, buf_b)

        @pl.when(n_steps % 2 == 1)
        def _():
            produce_and_consume(n_steps - 2, buf_b, None, n_steps - 3, buf_a)
            produce_and_consume(n_steps - 1, buf_a, last, n_steps - 2, buf_b)
            consume(n_steps - 1, buf_a)


def _sb_attn_kernel(q_ref, k_ref, v_ref, o_ref, y_a, y_b, carry_ref, acc_ref, qs_ref, *, n_pairs, n_pad, blk):
    lax.fori_loop(0, q_ref.shape[1] // blk, functools.partial(
        _sb_query_block, q_ref, k_ref, v_ref, o_ref, y_a, y_b, carry_ref, acc_ref, qs_ref, n_pairs, n_pad, blk), 0)


def _sb_query_block(q_ref, k_ref, v_ref, o_ref, y_a, y_b, carry_ref, acc_ref, qs_ref, n_pairs, n_pad, blk, qi, _):
    pairs = range(n_pairs)
    q_start = pl.multiple_of(qi * blk, blk)
    lo_half = lax.broadcasted_iota(jnp.int32, (1, LANES), 1) < HEAD_DIM
    for p in pairs:
        qs_ref[p] = _stack_halves(q_ref[0, pl.ds(q_start, blk), p * LANES:(p + 1) * LANES], lo_half)
    qs = [qs_ref.at[p] for p in pairs]
    row = lax.broadcasted_iota(jnp.int32, (2 * blk, blk), 0)
    col = lax.broadcasted_iota(jnp.int32, (2 * blk, blk), 1)
    key_j = lax.broadcasted_iota(jnp.int32, (blk, blk), 0)
    key_s = lax.broadcasted_iota(jnp.int32, (blk, blk), 1)
    neg_tri = jnp.where(key_j >= key_s, -1.0, 0.0).astype(BF16)
    past = col < row % blk
    real_key = col >= n_pad if n_pad else None

    def logits_pair(t, y_ref, mask, p):
        start = pl.multiple_of((qi - t) * blk, blk)
        kb = k_ref[0, pl.ds(start, blk), p * LANES:(p + 1) * LANES]
        y = lax.dot_general(qs[p][...], kb, _NT, preferred_element_type=F32)
        y_ref[p] = y if mask is None else jnp.where(mask, y, -jnp.inf)

    def logits(t, y_ref, mask):
        for p in pairs:
            logits_pair(t, y_ref, mask, p)

    def cum_pair(y_ref, p):
        y = y_ref[p]
        sp = jnp.maximum(y, 0.0) + jnp.log2(1.0 + jnp.exp2(_neg_abs(y)))
        return _dot(sp.astype(BF16), neg_tri)

    def weigh_pair(t, y_ref, p, cum):
        start = pl.multiple_of((qi - t) * blk, blk)
        vb = v_ref[0, pl.ds(start, blk), p * LANES:(p + 1) * LANES]
        carry = carry_ref[p]
        w = jnp.exp2((y_ref[p] + cum + jnp.concatenate([carry, carry], axis=1)).astype(BF16))
        acc_ref[p, 0] += _dot(w[:blk], vb)
        acc_ref[p, 1] += _dot(w[blk:], vb)
        carry_ref[p] = carry + jnp.broadcast_to(cum[:, 0:1], carry.shape)

    def absorb(t, y_ref):
        cum = [cum_pair(y_ref, p) for p in pairs]
        for p in pairs:
            weigh_pair(t, y_ref, p, cum[p])

    def logits_and_absorb(t_next, next_ref, mask, t, y_ref):
        cum = {}
        for p in pairs:
            logits_pair(t_next, next_ref, mask, p)
            cum[p] = cum_pair(y_ref, p)
            if p > 0:
                weigh_pair(t, y_ref, p - 1, cum[p - 1])
        weigh_pair(t, y_ref, n_pairs - 1, cum[n_pairs - 1])

    carry_ref[...] = jnp.zeros_like(carry_ref)
    acc_ref[...] = jnp.zeros_like(acc_ref)
    _visit_blocks(qi + 1, logits, absorb, logits_and_absorb, y_a, y_b,
                  first=past, last=real_key, only=_all_of(past, real_key))
    for p in pairs:
        o_ref[0, pl.ds(q_start, blk), p * LANES:(p + 1) * LANES] = jnp.where(
            lo_half, acc_ref[p, 0], acc_ref[p, 1]).astype(o_ref.dtype)
    return 0


def _sb_attention(qkv, batch, seq_len, n_pad):
    d = SB_HEADS * HEAD_DIM
    width = SB_PAIRS_PER_STEP * LANES
    groups = d // width
    qkv3 = qkv.reshape(batch, seq_len, 3 * d)
    out = pl.pallas_call(
        functools.partial(_sb_attn_kernel, n_pairs=SB_PAIRS_PER_STEP, n_pad=n_pad, blk=ATTN_BLOCK),
        grid=(batch, groups),
        in_specs=[
            pl.BlockSpec((1, seq_len, width), lambda b, g: (b, 0, g)),
            pl.BlockSpec((1, seq_len, width), lambda b, g: (b, 0, groups + g)),
            pl.BlockSpec((1, seq_len, width), lambda b, g: (b, 0, 2 * groups + g)),
        ],
        out_specs=pl.BlockSpec((1, seq_len, width), lambda b, g: (b, 0, g)),
        out_shape=jax.ShapeDtypeStruct((batch, seq_len, d), BF16),
        scratch_shapes=[pltpu.VMEM((SB_PAIRS_PER_STEP, 2 * ATTN_BLOCK, ATTN_BLOCK), F32)] * 2 + [
            pltpu.VMEM((SB_PAIRS_PER_STEP, 2 * ATTN_BLOCK, LANES), F32),
            pltpu.VMEM((SB_PAIRS_PER_STEP, 2, ATTN_BLOCK, LANES), F32),
            pltpu.VMEM((SB_PAIRS_PER_STEP, 2 * ATTN_BLOCK, LANES), BF16)],
        compiler_params=pltpu.CompilerParams(
            dimension_semantics=("parallel", "parallel"), vmem_limit_bytes=VMEM_LIMIT),
        name="sb_attention",
    )(qkv3, qkv3, qkv3)
    return out.reshape(batch * seq_len, d)


def _diff_attn_kernel(lam_ref, q_ref, k_ref, vt_ref, g_ref, o_ref, s_a, s_b, m_ref, l_ref, acc_ref, qs_ref, *,
                      n_heads, n_pad, lam_init, blk):
    lax.fori_loop(0, q_ref.shape[1] // blk, functools.partial(
        _diff_query_block, lam_ref, q_ref, k_ref, vt_ref, g_ref, o_ref, s_a, s_b, m_ref, l_ref, acc_ref, qs_ref,
        n_heads, n_pad, lam_init, blk), 0)


def _diff_query_block(lam_ref, q_ref, k_ref, vt_ref, g_ref, o_ref, s_a, s_b, m_ref, l_ref, acc_ref, qs_ref,
                      n_heads, n_pad, lam_init, blk, qi, _):
    heads = range(n_heads)
    q_start = pl.multiple_of(qi * blk, blk)
    lo_half = lax.broadcasted_iota(jnp.int32, (1, LANES), 1) < HEAD_DIM
    for h in heads:
        qs_ref[h] = _stack_halves(q_ref[0, pl.ds(q_start, blk), h * LANES:(h + 1) * LANES], lo_half)
    qs = [qs_ref.at[h] for h in heads]
    key = lax.broadcasted_iota(jnp.int32, (blk, 2 * blk), 0)
    qry = lax.broadcasted_iota(jnp.int32, (blk, 2 * blk), 1) % blk
    causal = key <= qry
    real_key = key >= n_pad if n_pad else None
    only = _all_of(causal, jnp.logical_or(real_key, qry < n_pad)) if n_pad else causal

    def scores_head(t, s_ref, mask, h):
        start = pl.multiple_of(t * blk, blk)
        kb = k_ref[0, pl.ds(start, blk), h * LANES:(h + 1) * LANES]
        sc = lax.dot_general(kb, qs[h][...], _NT, preferred_element_type=F32)
        s_ref[h] = sc if mask is None else jnp.where(mask, sc, -jnp.inf)

    def scores(t, s_ref, mask):
        for h in heads:
            scores_head(t, s_ref, mask, h)

    def softmax_head(s_ref, h):
        sh = s_ref[h]
        m = m_ref[h]
        m_new = jnp.maximum(m, jnp.max(sh, axis=0, keepdims=True))
        alpha = jnp.exp2(m - m_new)
        ph = jnp.exp2(sh - m_new)
        l_ref[h] = alpha * l_ref[h] + jnp.sum(ph, axis=0, keepdims=True)
        m_ref[h] = m_new
        return alpha, ph.astype(BF16)

    def accumulate(t, h, alpha, ph):
        acc_ref[h] = alpha * acc_ref[h] + _dot(vt_ref[0, t, h * LANES:(h + 1) * LANES, :], ph)

    def absorb(t, s_ref):
        for h in heads:
            accumulate(t, h, *softmax_head(s_ref, h))

    def scores_and_absorb(t_next, next_ref, mask, t, s_ref):
        scores_head(t_next, next_ref, mask, 0)
        for h in heads:
            alpha, ph = softmax_head(s_ref, h)
            if h + 1 < n_heads:
                scores_head(t_next, next_ref, mask, h + 1)
            accumulate(t, h, alpha, ph)

    m_ref[...] = jnp.full_like(m_ref, -jnp.inf)
    l_ref[...] = jnp.zeros_like(l_ref)
    acc_ref[...] = jnp.zeros_like(acc_ref)
    _visit_blocks(qi + 1, scores, absorb, scores_and_absorb, s_a, s_b, first=real_key, last=causal, only=only)

    lp = lam_ref[...]
    lam = (jnp.exp(jnp.sum(lp[0:1] * lp[1:2], axis=-1, keepdims=True))
           - jnp.exp(jnp.sum(lp[2:3] * lp[3:4], axis=-1, keepdims=True)) + lam_init)
    for h in heads:
        o = acc_ref[h] * (1.0 / l_ref[h])
        o = o[:, :blk] - lam * o[:, blk:]
        ms = jnp.mean(o * o, axis=0, keepdims=True)
        o = o * lax.rsqrt(ms + EPS) * (g_ref[...] * (1.0 - lam_init))
        o_ref[0, pl.ds(q_start, blk), h * LANES:(h + 1) * LANES] = o.T.astype(o_ref.dtype)
    return 0


def _diff_attention(q, k, vt, lam_params, subln_g, lam_init, batch, seq_len, n_pad):
    d = DIFF_HEADS * LANES
    width = DIFF_HEADS_PER_STEP * LANES
    groups = d // width
    n_blk = seq_len // ATTN_BLOCK
    q3 = q.reshape(batch, seq_len, d)
    k3 = k.reshape(batch, seq_len, d)
    vt4 = vt.reshape(batch, n_blk, d, ATTN_BLOCK)
    out = pl.pallas_call(
        functools.partial(_diff_attn_kernel, n_heads=DIFF_HEADS_PER_STEP, n_pad=n_pad, lam_init=lam_init,
                          blk=ATTN_BLOCK),
        grid=(batch, groups),
        in_specs=[
            pl.BlockSpec(lam_params.shape, lambda b, g: (0, 0)),
            pl.BlockSpec((1, seq_len, width), lambda b, g: (b, 0, g)),
            pl.BlockSpec((1, seq_len, width), lambda b, g: (b, 0, g)),
            pl.BlockSpec((1, n_blk, width, ATTN_BLOCK), lambda b, g: (b, 0, g, 0)),
            pl.BlockSpec((LANES, 1), lambda b, g: (0, 0)),
        ],
        out_specs=pl.BlockSpec((1, seq_len, width), lambda b, g: (b, 0, g)),
        out_shape=jax.ShapeDtypeStruct((batch, seq_len, d), BF16),
        scratch_shapes=[pltpu.VMEM((DIFF_HEADS_PER_STEP, ATTN_BLOCK, 2 * ATTN_BLOCK), F32)] * 2 + [
            pltpu.VMEM((DIFF_HEADS_PER_STEP, 1, 2 * ATTN_BLOCK), F32)] * 2 + [
            pltpu.VMEM((DIFF_HEADS_PER_STEP, LANES, 2 * ATTN_BLOCK), F32),
            pltpu.VMEM((DIFF_HEADS_PER_STEP, 2 * ATTN_BLOCK, LANES), BF16)],
        compiler_params=pltpu.CompilerParams(
            dimension_semantics=("parallel", "parallel"), vmem_limit_bytes=VMEM_LIMIT),
        name="diff_attention",
    )(lam_params, q3, k3, vt4, subln_g.reshape(LANES, 1))
    return out.reshape(batch * seq_len, d)


def _post_attn_kernel(h_ref, o_ref, wo_ref, g_ref, win_ref, wout_ref, out_ref):
    h1 = h_ref[...] + _dot(o_ref[...], wo_ref[...])
    hn = (h1 * _rms_scale(h1) * g_ref[...]).astype(BF16)
    acc = h1
    d_ff = win_ref.shape[1]
    for c0 in range(0, d_ff, FF_CHUNK):
        a = jnp.maximum(_dot(hn, win_ref[:, c0:c0 + FF_CHUNK]), 0.0)
        acc = acc + _dot((a * a).astype(BF16), wout_ref[c0:c0 + FF_CHUNK, :])
    out_ref[...] = acc


def _post_attention(h, o, wo, g, w_in, w_out, *, tm):
    n, d = h.shape
    row = pl.BlockSpec((tm, d), lambda i: (i, 0))
    return pl.pallas_call(
        _post_attn_kernel,
        grid=(n // tm,),
        in_specs=[row, pl.BlockSpec((tm, o.shape[1]), lambda i: (i, 0)), _resident(wo.shape),
                  _resident((1, d)), _resident(w_in.shape), _resident(w_out.shape)],
        out_specs=row,
        out_shape=jax.ShapeDtypeStruct((n, d), F32),
        compiler_params=pltpu.CompilerParams(
            dimension_semantics=("parallel",), vmem_limit_bytes=VMEM_LIMIT),
        name="post_attention",
    )(h, o, wo, g.reshape(1, d), w_in, w_out)


def _rope_tables(seq_len, n_pad):
    half = ROPE_DIM // 2
    inv_freq = jnp.power(ROPE_THETA, -jnp.arange(0, ROPE_DIM, 2, dtype=F32) / ROPE_DIM)
    pos = jnp.maximum(jnp.arange(seq_len) - n_pad, 0).astype(F32)
    ang = pos[:, None] * inv_freq[None, :]
    cos, sin = jnp.cos(ang), jnp.sin(ang)
    seg_lane = jnp.arange(LANES) % HEAD_DIM
    idx = seg_lane % half
    cos_t = jnp.where(seg_lane < ROPE_DIM, cos[:, idx], 1.0)
    s1_t = jnp.where(seg_lane < half, -sin[:, idx], 0.0)
    s2_t = jnp.where((seg_lane >= half) & (seg_lane < ROPE_DIM), sin[:, idx], 0.0)
    return cos_t, s1_t, s2_t


def kernel(x, meta_tokens, a_norm_g, a_w_qkv, a_w_o, kv_norm_g, kv_w, kv_k_norm_g, b_norm_g, b_w_q,
           b_q_norm_g, b_lambda, b_subln_g, b_w_o, mlp_norm_g, mlp_w_in, mlp_w_out):
    batch, s_len, d = x.shape
    n_a = a_w_qkv.shape[0]
    n_b = b_w_q.shape[0]
    real = s_len + N_META
    seq_len = -(-real // ATTN_BLOCK) * ATTN_BLOCK
    n_pad = seq_len - real
    n_rows = batch * seq_len
    tm = ROW_TILE if n_rows % ROW_TILE == 0 else ATTN_BLOCK

    meta = jnp.broadcast_to(meta_tokens.astype(x.dtype)[None], (batch, N_META, d))
    h = jnp.concatenate([jnp.zeros((batch, n_pad, d), x.dtype), meta, x], axis=1)
    h = h.reshape(n_rows, d)
    rope = _rope_tables(seq_len, n_pad)
    k_cols = DIFF_HEADS * 2 * HEAD_DIM
    qk_scale = HEAD_DIM ** -0.5 * math.log2(math.e)

    kv = None
    for layer in range(n_a + n_b):
        if layer < n_a:
            i = layer
            col_scale = jnp.concatenate([jnp.full((d,), qk_scale, F32), jnp.ones((2 * d,), F32)])
            w_qkv = (a_w_qkv[i] * col_scale).astype(BF16)
            qkv = _norm_proj(h, a_norm_g[i], w_qkv, tm=tm)
            o = _sb_attention(qkv, batch, seq_len, n_pad)
            w_o = a_w_o[i]
        else:
            j = layer - n_a
            if kv is None:
                k_gain = jnp.tile(kv_k_norm_g.reshape(2 * HEAD_DIM), DIFF_HEADS)
                kv = _norm_proj(h, kv_norm_g, kv_w[:, :k_cols].astype(BF16), tm=ATTN_BLOCK, seg_gain=k_gain,
                                rope=rope, n_rope_cols=k_cols, seq_len=seq_len,
                                w_t=kv_w[:, k_cols:].T.astype(BF16))
            lam_init = 0.8 - 0.6 * math.exp(-0.3 * layer)
            q_gain = jnp.tile(b_q_norm_g[j].reshape(2 * HEAD_DIM), DIFF_HEADS) * qk_scale
            q = _norm_proj(h, b_norm_g[j], b_w_q[j].astype(BF16), tm=ATTN_BLOCK, seg_gain=q_gain,
                           rope=rope, n_rope_cols=k_cols, seq_len=seq_len)
            o = _diff_attention(q, kv[0], kv[1], b_lambda[j].astype(F32), b_subln_g[j], lam_init, batch,
                                seq_len, n_pad)
            w_o = b_w_o[j]
        h = _post_attention(h, o, w_o.astype(BF16), mlp_norm_g[layer], mlp_w_in[layer].astype(BF16),
                            mlp_w_out[layer].astype(BF16), tm=tm)

    return h.reshape(batch, seq_len, d)[:, n_pad + N_META:]
```

```python
import functools
import math

import jax
import jax.numpy as jnp
from jax import lax
from jax.experimental import pallas as pl
from jax.experimental.pallas import tpu as pltpu

N_META = 16
SB_HEADS = 16
HEAD_DIM = 64
DIFF_HEADS = 8
ROPE_DIM = 16
ROPE_THETA = 500000.0
EPS = 1e-6

LANES = 128
ATTN_BLOCK = 256
ROW_TILE = 512
FF_CHUNK = 1024
SB_PAIRS_PER_STEP = 4
DIFF_HEADS_PER_STEP = 4
VMEM_LIMIT = 56 * 1024 * 1024

F32 = jnp.float32
BF16 = jnp.bfloat16
_NT = (((1,), (1,)), ((), ()))


def _dot(a, b):
    return jnp.dot(a, b, preferred_element_type=F32)


def _rms_scale(x):
    return lax.rsqrt(jnp.mean(x * x, axis=-1, keepdims=True) + EPS)


def _resident(shape):
    nd = len(shape)
    return pl.BlockSpec(shape, lambda *_: (0,) * nd, pipeline_mode=pl.Buffered(1))


def _norm_proj_kernel(h_ref, g_ref, w_ref, *rest, n_rope_cols, transposed_extra):
    wt_ref = ot_ref = None
    if transposed_extra:
        wt_ref, *rest, ot_ref = rest
    if n_rope_cols:
        sg_ref, cos_ref, s1_ref, s2_ref, o_ref = rest
    else:
        (o_ref,) = rest
    x = h_ref[...]
    hn = (x * _rms_scale(x) * g_ref[...]).astype(BF16)
    n_out = o_ref.shape[1]
    if n_rope_cols:
        r = lax.broadcasted_iota(jnp.int32, (LANES, LANES), 0) // HEAD_DIM
        c = lax.broadcasted_iota(jnp.int32, (LANES, LANES), 1) // HEAD_DIM
        seg = jnp.where(r == c, 1.0, 0.0).astype(BF16)
        cos, s1, s2 = cos_ref[...], s1_ref[...], s2_ref[...]
    chunk = min(FF_CHUNK, n_out)
    for c0 in range(0, n_out, chunk):
        y = _dot(hn, w_ref[:, c0:c0 + chunk])
        if c0 >= n_rope_cols:
            o_ref[:, c0:c0 + chunk] = y.astype(o_ref.dtype)
            continue
        for l0 in range(0, chunk, LANES):
            yc = y[:, l0:l0 + LANES]
            ms = _dot((yc * yc).astype(BF16), seg) * (1.0 / HEAD_DIM)
            yn = yc * lax.rsqrt(ms + EPS) * sg_ref[:, c0 + l0:c0 + l0 + LANES]
            rot = (yn * cos
                   + pltpu.roll(yn, LANES - ROPE_DIM // 2, 1) * s1
                   + pltpu.roll(yn, ROPE_DIM // 2, 1) * s2)
            o_ref[:, c0 + l0:c0 + l0 + LANES] = rot.astype(o_ref.dtype)
    if transposed_extra:
        ot_ref[0] = lax.dot_general(wt_ref[...], hn, _NT, preferred_element_type=F32).astype(ot_ref.dtype)


def _norm_proj(h, g, w, *, tm, seg_gain=None, rope=None, n_rope_cols=0, seq_len=None, w_t=None):
    n, d = h.shape
    n_out = w.shape[1]
    in_specs = [pl.BlockSpec((tm, d), lambda i: (i, 0)), _resident((1, d)), _resident((d, n_out))]
    args = [h, g.reshape(1, d), w]
    if w_t is not None:
        in_specs.append(_resident(w_t.shape))
        args.append(w_t)
    if n_rope_cols:
        tiles_per_seq = seq_len // tm
        tab = pl.BlockSpec((tm, LANES), lambda i: (i % tiles_per_seq, 0))
        in_specs += [_resident((1, n_rope_cols)), tab, tab, tab]
        args += [seg_gain.reshape(1, n_rope_cols), *rope]
    out_specs = pl.BlockSpec((tm, n_out), lambda i: (i, 0))
    out_shape = jax.ShapeDtypeStruct((n, n_out), BF16)
    if w_t is not None:
        out_specs = [out_specs, pl.BlockSpec((1, w_t.shape[0], tm), lambda i: (i, 0, 0))]
        out_shape = [out_shape, jax.ShapeDtypeStruct((n // tm, w_t.shape[0], tm), BF16)]
    return pl.pallas_call(
        functools.partial(_norm_proj_kernel, n_rope_cols=n_rope_cols, transposed_extra=w_t is not None),
        grid=(n // tm,),
        in_specs=in_specs,
        out_specs=out_specs,
        out_shape=out_shape,
        compiler_params=pltpu.CompilerParams(
            dimension_semantics=("parallel",), vmem_limit_bytes=VMEM_LIMIT),
        name="norm_proj_rope" if n_rope_cols else "norm_proj",
    )(*args)


def _stack_halves(q, lo_half):
    zero = jnp.zeros_like(q)
    return jnp.concatenate([jnp.where(lo_half, q, zero), jnp.where(lo_half, zero, q)], axis=0)


def _neg_abs(x):
    sign = jnp.uint32(0x80000000)
    return lax.bitcast_convert_type(lax.bitcast_convert_type(x, jnp.uint32) | sign, F32)


def _all_of(*masks):
    masks = [m for m in masks if m is not None]
    return functools.reduce(jnp.logical_and, masks) if masks else None


def _visit_blocks(n_steps, produce, consume, produce_and_consume, buf_a, buf_b, first, last, only):
    @pl.when(n_steps == 1)
    def _():
        produce(0, buf_a, only)
        consume(0, buf_a)

    @pl.when(n_steps > 1)
    def _():
        produce(0, buf_a, first)

        def two_steps(k, _):
            t = 2 * k
            produce_and_consume(t + 1, buf_b, None, t, buf_a)
            produce_and_consume(t + 2, buf_a, None, t + 1, buf_b)
            return 0

        lax.fori_loop(0, (n_steps - 2) // 2, two_steps, 0)

        @pl.when(n_steps % 2 == 0)
        def _():
            produce_and_consume(n_steps - 1, buf_b, last, n_steps - 2, buf_a)
            consume(n_steps - 1, buf_b)

        @pl.when(n_steps % 2 == 1)
        def _():
            produce_and_consume(n_steps - 2, buf_b, None, n_steps - 3, buf_a)
            produce_and_consume(n_steps - 1, buf_a, last, n_steps - 2, buf_b)
            consume(n_steps - 1, buf_a)


def _sb_attn_kernel(q_ref, k_ref, v_ref, o_ref, y_a, y_b, carry_ref, acc_ref, qs_ref, *, n_pairs, n_pad, blk):
    lax.fori_loop(0, q_ref.shape[1] // blk, functools.partial(
        _sb_query_block, q_ref, k_ref, v_ref, o_ref, y_a, y_b, carry_ref, acc_ref, qs_ref, n_pairs, n_pad, blk), 0)


def _sb_query_block(q_ref, k_ref, v_ref, o_ref, y_a, y_b, carry_ref, acc_ref, qs_ref, n_pairs, n_pad, blk, qi, _):
    pairs = range(n_pairs)
    q_start = pl.multiple_of(qi * blk, blk)
    lo_half = lax.broadcasted_iota(jnp.int32, (1, LANES), 1) < HEAD_DIM
    for p in pairs:
        qs_ref[p] = _stack_halves(q_ref[0, pl.ds(q_start, blk), p * LANES:(p + 1) * LANES], lo_half)
    qs = [qs_ref.at[p] for p in pairs]
    row = lax.broadcasted_iota(jnp.int32, (2 * blk, blk), 0)
    col = lax.broadcasted_iota(jnp.int32, (2 * blk, blk), 1)
    key_j = lax.broadcasted_iota(jnp.int32, (blk, blk), 0)
    key_s = lax.broadcasted_iota(jnp.int32, (blk, blk), 1)
    neg_tri = jnp.where(key_j >= key_s, -1.0, 0.0).astype(BF16)
    past = col < row % blk
    real_key = col >= n_pad if n_pad else None

    def logits_pair(t, y_ref, mask, p):
        start = pl.multiple_of((qi - t) * blk, blk)
        kb = k_ref[0, pl.ds(start, blk), p * LANES:(p + 1) * LANES]
        y = lax.dot_general(qs[p][...], kb, _NT, preferred_element_type=F32)
        y_ref[p] = y if mask is None else jnp.where(mask, y, -jnp.inf)

    def logits(t, y_ref, mask):
        for p in pairs:
            logits_pair(t, y_ref, mask, p)

    def cum_pair(y_ref, p):
        y = y_ref[p]
        sp = jnp.maximum(y, 0.0) + jnp.log2(1.0 + jnp.exp2(_neg_abs(y)))
        return _dot(sp.astype(BF16), neg_tri)

    def weigh_pair(t, y_ref, p, cum):
        start = pl.multiple_of((qi - t) * blk, blk)
        vb = v_ref[0, pl.ds(start, blk), p * LANES:(p + 1) * LANES]
        carry = carry_ref[p]
        w = jnp.exp2((y_ref[p] + cum + jnp.concatenate([carry, carry], axis=1)).astype(BF16))
        acc_ref[p, 0] += _dot(w[:blk], vb)
        acc_ref[p, 1] += _dot(w[blk:], vb)
        carry_ref[p] = carry + jnp.broadcast_to(cum[:, 0:1], carry.shape)

    def absorb(t, y_ref):
        cum = [cum_pair(y_ref, p) for p in pairs]
        for p in pairs:
            weigh_pair(t, y_ref, p, cum[p])

    def logits_and_absorb(t_next, next_ref, mask, t, y_ref):
        cum = {}
        for p in pairs:
            logits_pair(t_next, next_ref, mask, p)
            cum[p] = cum_pair(y_ref, p)
            if p > 0:
                weigh_pair(t, y_ref, p - 1, cum[p - 1])
        weigh_pair(t, y_ref, n_pairs - 1, cum[n_pairs - 1])

    carry_ref[...] = jnp.zeros_like(carry_ref)
    acc_ref[...] = jnp.zeros_like(acc_ref)
    _visit_blocks(qi + 1, logits, absorb, logits_and_absorb, y_a, y_b,
                  first=past, last=real_key, only=_all_of(past, real_key))
    for p in pairs:
        o_ref[0, pl.ds(q_start, blk), p * LANES:(p + 1) * LANES] = jnp.where(
            lo_half, acc_ref[p, 0], acc_ref[p, 1]).astype(o_ref.dtype)
    return 0


def _sb_attention(qkv, batch, seq_len, n_pad):
    d = SB_HEADS * HEAD_DIM
    width = SB_PAIRS_PER_STEP * LANES
    groups = d // width
    qkv3 = qkv.reshape(batch, seq_len, 3 * d)
    out = pl.pallas_call(
        functools.partial(_sb_attn_kernel, n_pairs=SB_PAIRS_PER_STEP, n_pad=n_pad, blk=ATTN_BLOCK),
        grid=(batch, groups),
        in_specs=[
            pl.BlockSpec((1, seq_len, width), lambda b, g: (b, 0, g)),
            pl.BlockSpec((1, seq_len, width), lambda b, g: (b, 0, groups + g)),
            pl.BlockSpec((1, seq_len, width), lambda b, g: (b, 0, 2 * groups + g)),
        ],
        out_specs=pl.BlockSpec((1, seq_len, width), lambda b, g: (b, 0, g)),
        out_shape=jax.ShapeDtypeStruct((batch, seq_len, d), BF16),
        scratch_shapes=[pltpu.VMEM((SB_PAIRS_PER_STEP, 2 * ATTN_BLOCK, ATTN_BLOCK), F32)] * 2 + [
            pltpu.VMEM((SB_PAIRS_PER_STEP, 2 * ATTN_BLOCK, LANES), F32),
            pltpu.VMEM((SB_PAIRS_PER_STEP, 2, ATTN_BLOCK, LANES), F32),
            pltpu.VMEM((SB_PAIRS_PER_STEP, 2 * ATTN_BLOCK, LANES), BF16)],
        compiler_params=pltpu.CompilerParams(
            dimension_semantics=("parallel", "parallel"), vmem_limit_bytes=VMEM_LIMIT),
        name="sb_attention",
    )(qkv3, qkv3, qkv3)
    return out.reshape(batch * seq_len, d)


def _diff_attn_kernel(lam_ref, q_ref, k_ref, vt_ref, g_ref, o_ref, s_a, s_b, m_ref, l_ref, acc_ref, qs_ref, *,
                      n_heads, n_pad, lam_init, blk):
    lax.fori_loop(0, q_ref.shape[1] // blk, functools.partial(
        _diff_query_block, lam_ref, q_ref, k_ref, vt_ref, g_ref, o_ref, s_a, s_b, m_ref, l_ref, acc_ref, qs_ref,
        n_heads, n_pad, lam_init, blk), 0)


def _diff_query_block(lam_ref, q_ref, k_ref, vt_ref, g_ref, o_ref, s_a, s_b, m_ref, l_ref, acc_ref, qs_ref,
                      n_heads, n_pad, lam_init, blk, qi, _):
    units = range(2 * n_heads)
    q_start = pl.multiple_of(qi * blk, blk)
    lo_half = lax.broadcasted_iota(jnp.int32, (1, LANES), 1) < HEAD_DIM
    for h in range(n_heads):
        q = q_ref[0, pl.ds(q_start, blk), h * LANES:(h + 1) * LANES]
        zero = jnp.zeros_like(q)
        qs_ref[2 * h] = jnp.where(lo_half, q, zero)
        qs_ref[2 * h + 1] = jnp.where(lo_half, zero, q)
    key = lax.broadcasted_iota(jnp.int32, (blk, blk), 0)
    qry = lax.broadcasted_iota(jnp.int32, (blk, blk), 1)
    causal = key <= qry
    real_key = key >= n_pad if n_pad else None
    only = _all_of(causal, jnp.logical_or(real_key, qry < n_pad)) if n_pad else causal

    def scores_unit(t, s_ref, mask, u):
        h = u // 2
        start = pl.multiple_of(t * blk, blk)
        kb = k_ref[0, pl.ds(start, blk), h * LANES:(h + 1) * LANES]
        sc = lax.dot_general(kb, qs_ref[u], _NT, preferred_element_type=F32)
        s_ref[u] = sc if mask is None else jnp.where(mask, sc, -jnp.inf)

    def scores(t, s_ref, mask):
        for u in units:
            scores_unit(t, s_ref, mask, u)

    def softmax_unit(s_ref, u):
        sh = s_ref[u]
        m = m_ref[u]
        m_new = jnp.maximum(m, jnp.max(sh, axis=0, keepdims=True))
        alpha = jnp.exp2(m - m_new)
        ph = jnp.exp2(sh - m_new)
        l_ref[u] = alpha * l_ref[u] + jnp.sum(ph, axis=0, keepdims=True)
        m_ref[u] = m_new
        return alpha, ph.astype(BF16)

    def accumulate(t, u, alpha, ph):
        h = u // 2
        acc_ref[u] = alpha * acc_ref[u] + _dot(vt_ref[0, t, h * LANES:(h + 1) * LANES, :], ph)

    def absorb(t, s_ref):
        for u in units:
            accumulate(t, u, *softmax_unit(s_ref, u))

    def scores_and_absorb(t_next, next_ref, mask, t, s_ref):
        scores_unit(t_next, next_ref, mask, 0)
        for u in units:
            alpha, ph = softmax_unit(s_ref, u)
            if u + 1 < 2 * n_heads:
                scores_unit(t_next, next_ref, mask, u + 1)
            accumulate(t, u, alpha, ph)

    m_ref[...] = jnp.full_like(m_ref, -jnp.inf)
    l_ref[...] = jnp.zeros_like(l_ref)
    acc_ref[...] = jnp.zeros_like(acc_ref)
    _visit_blocks(qi + 1, scores, absorb, scores_and_absorb, s_a, s_b, first=real_key, last=causal, only=only)

    lp = lam_ref[...]
    lam = (jnp.exp(jnp.sum(lp[0:1] * lp[1:2], axis=-1, keepdims=True))
           - jnp.exp(jnp.sum(lp[2:3] * lp[3:4], axis=-1, keepdims=True)) + lam_init)
    for h in range(n_heads):
        o = (acc_ref[2 * h] * (1.0 / l_ref[2 * h])
             - lam * (acc_ref[2 * h + 1] * (1.0 / l_ref[2 * h + 1])))
        ms = jnp.mean(o * o, axis=0, keepdims=True)
        o = o * lax.rsqrt(ms + EPS) * (g_ref[...] * (1.0 - lam_init))
        o_ref[0, pl.ds(q_start, blk), h * LANES:(h + 1) * LANES] = o.T.astype(o_ref.dtype)
    return 0


def _diff_attention(q, k, vt, lam_params, subln_g, lam_init, batch, seq_len, n_pad):
    d = DIFF_HEADS * LANES
    width = DIFF_HEADS_PER_STEP * LANES
    groups = d // width
    n_blk = seq_len // ATTN_BLOCK
    q3 = q.reshape(batch, seq_len, d)
    k3 = k.reshape(batch, seq_len, d)
    vt4 = vt.reshape(batch, n_blk, d, ATTN_BLOCK)
    out = pl.pallas_call(
        functools.partial(_diff_attn_kernel, n_heads=DIFF_HEADS_PER_STEP, n_pad=n_pad, lam_init=lam_init,
                          blk=ATTN_BLOCK),
        grid=(batch, groups),
        in_specs=[
            pl.BlockSpec(lam_params.shape, lambda b, g: (0, 0)),
            pl.BlockSpec((1, seq_len, width), lambda b, g: (b, 0, g)),
            pl.BlockSpec((1, seq_len, width), lambda b, g: (b, 0, g)),
            pl.BlockSpec((1, n_blk, width, ATTN_BLOCK), lambda b, g: (b, 0, g, 0)),
            pl.BlockSpec((LANES, 1), lambda b, g: (0, 0)),
        ],
        out_specs=pl.BlockSpec((1, seq_len, width), lambda b, g: (b, 0, g)),
        out_shape=jax.ShapeDtypeStruct((batch, seq_len, d), BF16),
        scratch_shapes=[pltpu.VMEM((2 * DIFF_HEADS_PER_STEP, ATTN_BLOCK, ATTN_BLOCK), F32)] * 2 + [
            pltpu.VMEM((2 * DIFF_HEADS_PER_STEP, 1, ATTN_BLOCK), F32)] * 2 + [
            pltpu.VMEM((2 * DIFF_HEADS_PER_STEP, LANES, ATTN_BLOCK), F32),
            pltpu.VMEM((2 * DIFF_HEADS_PER_STEP, ATTN_BLOCK, LANES), BF16)],
        compiler_params=pltpu.CompilerParams(
            dimension_semantics=("parallel", "parallel"), vmem_limit_bytes=VMEM_LIMIT),
        name="diff_attention",
    )(lam_params, q3, k3, vt4, subln_g.reshape(LANES, 1))
    return out.reshape(batch * seq_len, d)


def _post_attn_kernel(h_ref, o_ref, wo_ref, g_ref, win_ref, wout_ref, out_ref):
    h1 = h_ref[...] + _dot(o_ref[...], wo_ref[...])
    hn = (h1 * _rms_scale(h1) * g_ref[...]).astype(BF16)
    acc = h1
    d_ff = win_ref.shape[1]
    for c0 in range(0, d_ff, FF_CHUNK):
        a = jnp.maximum(_dot(hn, win_ref[:, c0:c0 + FF_CHUNK]), 0.0)
        acc = acc + _dot((a * a).astype(BF16), wout_ref[c0:c0 + FF_CHUNK, :])
    out_ref[...] = acc


def _post_attention(h, o, wo, g, w_in, w_out, *, tm):
    n, d = h.shape
    row = pl.BlockSpec((tm, d), lambda i: (i, 0))
    return pl.pallas_call(
        _post_attn_kernel,
        grid=(n // tm,),
        in_specs=[row, pl.BlockSpec((tm, o.shape[1]), lambda i: (i, 0)), _resident(wo.shape),
                  _resident((1, d)), _resident(w_in.shape), _resident(w_out.shape)],
        out_specs=row,
        out_shape=jax.ShapeDtypeStruct((n, d), F32),
        compiler_params=pltpu.CompilerParams(
            dimension_semantics=("parallel",), vmem_limit_bytes=VMEM_LIMIT),
        name="post_attention",
    )(h, o, wo, g.reshape(1, d), w_in, w_out)


def _rope_tables(seq_len, n_pad):
    half = ROPE_DIM // 2
    inv_freq = jnp.power(ROPE_THETA, -jnp.arange(0, ROPE_DIM, 2, dtype=F32) / ROPE_DIM)
    pos = jnp.maximum(jnp.arange(seq_len) - n_pad, 0).astype(F32)
    ang = pos[:, None] * inv_freq[None, :]
    cos, sin = jnp.cos(ang), jnp.sin(ang)
    seg_lane = jnp.arange(LANES) % HEAD_DIM
    idx = seg_lane % half
    cos_t = jnp.where(seg_lane < ROPE_DIM, cos[:, idx], 1.0)
    s1_t = jnp.where(seg_lane < half, -sin[:, idx], 0.0)
    s2_t = jnp.where((seg_lane >= half) & (seg_lane < ROPE_DIM), sin[:, idx], 0.0)
    return cos_t, s1_t, s2_t


def kernel(x, meta_tokens, a_norm_g, a_w_qkv, a_w_o, kv_norm_g, kv_w, kv_k_norm_g, b_norm_g, b_w_q,
           b_q_norm_g, b_lambda, b_subln_g, b_w_o, mlp_norm_g, mlp_w_in, mlp_w_out):
    batch, s_len, d = x.shape
    n_a = a_w_qkv.shape[0]
    n_b = b_w_q.shape[0]
    real = s_len + N_META
    seq_len = -(-real // ATTN_BLOCK) * ATTN_BLOCK
    n_pad = seq_len - real
    n_rows = batch * seq_len
    tm = ROW_TILE if n_rows % ROW_TILE == 0 else ATTN_BLOCK

    meta = jnp.broadcast_to(meta_tokens.astype(x.dtype)[None], (batch, N_META, d))
    h = jnp.concatenate([jnp.zeros((batch, n_pad, d), x.dtype), meta, x], axis=1)
    h = h.reshape(n_rows, d)
    rope = _rope_tables(seq_len, n_pad)
    k_cols = DIFF_HEADS * 2 * HEAD_DIM
    qk_scale = HEAD_DIM ** -0.5 * math.log2(math.e)

    kv = None
    for layer in range(n_a + n_b):
        if layer < n_a:
            i = layer
            col_scale = jnp.concatenate([jnp.full((d,), qk_scale, F32), jnp.ones((2 * d,), F32)])
            w_qkv = (a_w_qkv[i] * col_scale).astype(BF16)
            qkv = _norm_proj(h, a_norm_g[i], w_qkv, tm=tm)
            o = _sb_attention(qkv, batch, seq_len, n_pad)
            w_o = a_w_o[i]
        else:
            j = layer - n_a
            if kv is None:
                k_gain = jnp.tile(kv_k_norm_g.reshape(2 * HEAD_DIM), DIFF_HEADS)
                kv = _norm_proj(h, kv_norm_g, kv_w[:, :k_cols].astype(BF16), tm=ATTN_BLOCK, seg_gain=k_gain,
                                rope=rope, n_rope_cols=k_cols, seq_len=seq_len,
                                w_t=kv_w[:, k_cols:].T.astype(BF16))
            lam_init = 0.8 - 0.6 * math.exp(-0.3 * layer)
            q_gain = jnp.tile(b_q_norm_g[j].reshape(2 * HEAD_DIM), DIFF_HEADS) * qk_scale
            q = _norm_proj(h, b_norm_g[j], b_w_q[j].astype(BF16), tm=ATTN_BLOCK, seg_gain=q_gain,
                           rope=rope, n_rope_cols=k_cols, seq_len=seq_len)
            o = _diff_attention(q, kv[0], kv[1], b_lambda[j].astype(F32), b_subln_g[j], lam_init, batch,
                                seq_len, n_pad)
            w_o = b_w_o[j]
        h = _post_attention(h, o, w_o.astype(BF16), mlp_norm_g[layer], mlp_w_in[layer].astype(BF16),
                            mlp_w_out[layer].astype(BF16), tm=tm)

    return h.reshape(batch, seq_len, d)[:, n_pad + N_META:]
```

```python
import functools
import math

import jax
import jax.numpy as jnp
from jax import lax
from jax.experimental import pallas as pl
from jax.experimental.pallas import tpu as pltpu

N_META = 16
SB_HEADS = 16
HEAD_DIM = 64
DIFF_HEADS = 8
ROPE_DIM = 16
ROPE_THETA = 500000.0
EPS = 1e-6

LANES = 128
ATTN_BLOCK = 256
ROW_TILE = 512
FF_CHUNK = 1024
SB_PAIRS_PER_STEP = 4
DIFF_HEADS_PER_STEP = 4
VMEM_LIMIT = 56 * 1024 * 1024

F32 = jnp.float32
BF16 = jnp.bfloat16
_NT = (((1,), (1,)), ((), ()))


def _dot(a, b):
    return jnp.dot(a, b, preferred_element_type=F32)


def _rms_scale(x):
    return lax.rsqrt(jnp.mean(x * x, axis=-1, keepdims=True) + EPS)


def _resident(shape):
    nd = len(shape)
    return pl.BlockSpec(shape, lambda *_: (0,) * nd, pipeline_mode=pl.Buffered(1))


def _norm_proj_kernel(h_ref, g_ref, w_ref, *rest, n_rope_cols, transposed_extra):
    wt_ref = ot_ref = None
    if transposed_extra:
        wt_ref, *rest, ot_ref = rest
    if n_rope_cols:
        sg_ref, cos_ref, s1_ref, s2_ref, o_ref = rest
    else:
        (o_ref,) = rest
    x = h_ref[...]
    hn = (x * _rms_scale(x) * g_ref[...]).astype(BF16)
    n_out = o_ref.shape[1]
    if n_rope_cols:
        r = lax.broadcasted_iota(jnp.int32, (LANES, LANES), 0) // HEAD_DIM
        c = lax.broadcasted_iota(jnp.int32, (LANES, LANES), 1) // HEAD_DIM
        seg = jnp.where(r == c, 1.0, 0.0).astype(BF16)
        cos, s1, s2 = cos_ref[...], s1_ref[...], s2_ref[...]
    chunk = min(FF_CHUNK, n_out)
    for c0 in range(0, n_out, chunk):
        y = _dot(hn, w_ref[:, c0:c0 + chunk])
        if c0 >= n_rope_cols:
            o_ref[:, c0:c0 + chunk] = y.astype(o_ref.dtype)
            continue
        for l0 in range(0, chunk, LANES):
            yc = y[:, l0:l0 + LANES]
            ms = _dot((yc * yc).astype(BF16), seg) * (1.0 / HEAD_DIM)
            yn = yc * lax.rsqrt(ms + EPS) * sg_ref[:, c0 + l0:c0 + l0 + LANES]
            rot = (yn * cos
                   + pltpu.roll(yn, LANES - ROPE_DIM // 2, 1) * s1
                   + pltpu.roll(yn, ROPE_DIM // 2, 1) * s2)
            o_ref[:, c0 + l0:c0 + l0 + LANES] = rot.astype(o_ref.dtype)
    if transposed_extra:
        ot_ref[0] = lax.dot_general(wt_ref[...], hn, _NT, preferred_element_type=F32).astype(ot_ref.dtype)


def _norm_proj(h, g, w, *, tm, seg_gain=None, rope=None, n_rope_cols=0, seq_len=None, w_t=None):
    n, d = h.shape
    n_out = w.shape[1]
    in_specs = [pl.BlockSpec((tm, d), lambda i: (i, 0)), _resident((1, d)), _resident((d, n_out))]
    args = [h, g.reshape(1, d), w]
    if w_t is not None:
        in_specs.append(_resident(w_t.shape))
        args.append(w_t)
    if n_rope_cols:
        tiles_per_seq = seq_len // tm
        tab = pl.BlockSpec((tm, LANES), lambda i: (i % tiles_per_seq, 0))
        in_specs += [_resident((1, n_rope_cols)), tab, tab, tab]
        args += [seg_gain.reshape(1, n_rope_cols), *rope]
    out_specs = pl.BlockSpec((tm, n_out), lambda i: (i, 0))
    out_shape = jax.ShapeDtypeStruct((n, n_out), BF16)
    if w_t is not None:
        out_specs = [out_specs, pl.BlockSpec((1, w_t.shape[0], tm), lambda i: (i, 0, 0))]
        out_shape = [out_shape, jax.ShapeDtypeStruct((n // tm, w_t.shape[0], tm), BF16)]
    return pl.pallas_call(
        functools.partial(_norm_proj_kernel, n_rope_cols=n_rope_cols, transposed_extra=w_t is not None),
        grid=(n // tm,),
        in_specs=in_specs,
        out_specs=out_specs,
        out_shape=out_shape,
        compiler_params=pltpu.CompilerParams(
            dimension_semantics=("parallel",), vmem_limit_bytes=VMEM_LIMIT),
        name="norm_proj_rope" if n_rope_cols else "norm_proj",
    )(*args)


def _stack_halves(q, lo_half):
    zero = jnp.zeros_like(q)
    return jnp.concatenate([jnp.where(lo_half, q, zero), jnp.where(lo_half, zero, q)], axis=0)


def _neg_abs(x):
    sign = jnp.uint32(0x80000000)
    return lax.bitcast_convert_type(lax.bitcast_convert_type(x, jnp.uint32) | sign, F32)


def _all_of(*masks):
    masks = [m for m in masks if m is not None]
    return functools.reduce(jnp.logical_and, masks) if masks else None


def _visit_blocks(n_steps, produce, consume, produce_and_consume, buf_a, buf_b, first, last, only):
    @pl.when(n_steps == 1)
    def _():
        produce(0, buf_a, only)
        consume(0, buf_a)

    @pl.when(n_steps > 1)
    def _():
        produce(0, buf_a, first)

        def two_steps(k, _):
            t = 2 * k
            produce_and_consume(t + 1, buf_b, None, t, buf_a)
            produce_and_consume(t + 2, buf_a, None, t + 1, buf_b)
            return 0

        lax.fori_loop(0, (n_steps - 2) // 2, two_steps, 0)

        @pl.when(n_steps % 2 == 0)
        def _():
            produce_and_consume(n_steps - 1, buf_b, last, n_steps - 2, buf_a)
            consume(n_steps - 1, buf_b)

        @pl.when(n_steps % 2 == 1)
        def _():
            produce_and_consume(n_steps - 2, buf_b, None, n_steps - 3, buf_a)
            produce_and_consume(n_steps - 1, buf_a, last, n_steps - 2, buf_b)
            consume(n_steps - 1, buf_a)


def _sb_attn_kernel(q_ref, k_ref, v_ref, o_ref, y_a, y_b, carry_ref, acc_ref, qs_ref, *, n_pairs, n_pad, blk):
    lax.fori_loop(0, q_ref.shape[1] // blk, functools.partial(
        _sb_query_block, q_ref, k_ref, v_ref, o_ref, y_a, y_b, carry_ref, acc_ref, qs_ref, n_pairs, n_pad, blk), 0)


def _sb_query_block(q_ref, k_ref, v_ref, o_ref, y_a, y_b, carry_ref, acc_ref, qs_ref, n_pairs, n_pad, blk, qi, _):
    units = range(2 * n_pairs)
    q_start = pl.multiple_of(qi * blk, blk)
    lo_half = lax.broadcasted_iota(jnp.int32, (1, LANES), 1) < HEAD_DIM
    for p in range(n_pairs):
        q = q_ref[0, pl.ds(q_start, blk), p * LANES:(p + 1) * LANES]
        zero = jnp.zeros_like(q)
        qs_ref[2 * p] = jnp.where(lo_half, q, zero)
        qs_ref[2 * p + 1] = jnp.where(lo_half, zero, q)
    row = lax.broadcasted_iota(jnp.int32, (blk, blk), 0)
    col = lax.broadcasted_iota(jnp.int32, (blk, blk), 1)
    neg_tri = jnp.where(row >= col, -1.0, 0.0).astype(BF16)
    past = col < row
    real_key = col >= n_pad if n_pad else None

    def logits_unit(t, y_ref, mask, u):
        p = u // 2
        start = pl.multiple_of((qi - t) * blk, blk)
        kb = k_ref[0, pl.ds(start, blk), p * LANES:(p + 1) * LANES]
        y = lax.dot_general(qs_ref[u], kb, _NT, preferred_element_type=F32)
        y_ref[u] = y if mask is None else jnp.where(mask, y, -jnp.inf)

    def logits(t, y_ref, mask):
        for u in units:
            logits_unit(t, y_ref, mask, u)

    def cum_unit(y_ref, u):
        y = y_ref[u]
        sp = jnp.maximum(y, 0.0) + jnp.log2(1.0 + jnp.exp2(_neg_abs(y)))
        return _dot(sp.astype(BF16), neg_tri)

    def weigh_unit(t, y_ref, u, cum):
        p = u // 2
        start = pl.multiple_of((qi - t) * blk, blk)
        vb = v_ref[0, pl.ds(start, blk), p * LANES:(p + 1) * LANES]
        carry = carry_ref[u]
        w = jnp.exp2((y_ref[u] + cum + jnp.concatenate([carry, carry], axis=1)).astype(BF16))
        acc_ref[u] += _dot(w, vb)
        carry_ref[u] = carry + jnp.broadcast_to(cum[:, 0:1], carry.shape)

    def absorb(t, y_ref):
        cum = [cum_unit(y_ref, u) for u in units]
        for u in units:
            weigh_unit(t, y_ref, u, cum[u])

    def logits_and_absorb(t_next, next_ref, mask, t, y_ref):
        cum = {}
        for u in units:
            logits_unit(t_next, next_ref, mask, u)
            cum[u] = cum_unit(y_ref, u)
            if u > 0:
                weigh_unit(t, y_ref, u - 1, cum[u - 1])
        weigh_unit(t, y_ref, 2 * n_pairs - 1, cum[2 * n_pairs - 1])

    carry_ref[...] = jnp.zeros_like(carry_ref)
    acc_ref[...] = jnp.zeros_like(acc_ref)
    _visit_blocks(qi + 1, logits, absorb, logits_and_absorb, y_a, y_b,
                  first=past, last=real_key, only=_all_of(past, real_key))
    for p in range(n_pairs):
        o_ref[0, pl.ds(q_start, blk), p * LANES:(p + 1) * LANES] = jnp.where(
            lo_half, acc_ref[2 * p], acc_ref[2 * p + 1]).astype(o_ref.dtype)
    return 0


def _sb_attention(qkv, batch, seq_len, n_pad):
    d = SB_HEADS * HEAD_DIM
    width = SB_PAIRS_PER_STEP * LANES
    groups = d // width
    qkv3 = qkv.reshape(batch, seq_len, 3 * d)
    out = pl.pallas_call(
        functools.partial(_sb_attn_kernel, n_pairs=SB_PAIRS_PER_STEP, n_pad=n_pad, blk=ATTN_BLOCK),
        grid=(batch, groups),
        in_specs=[
            pl.BlockSpec((1, seq_len, width), lambda b, g: (b, 0, g)),
            pl.BlockSpec((1, seq_len, width), lambda b, g: (b, 0, groups + g)),
            pl.BlockSpec((1, seq_len, width), lambda b, g: (b, 0, 2 * groups + g)),
        ],
        out_specs=pl.BlockSpec((1, seq_len, width), lambda b, g: (b, 0, g)),
        out_shape=jax.ShapeDtypeStruct((batch, seq_len, d), BF16),
        scratch_shapes=[pltpu.VMEM((2 * SB_PAIRS_PER_STEP, ATTN_BLOCK, ATTN_BLOCK), F32)] * 2 + [
            pltpu.VMEM((2 * SB_PAIRS_PER_STEP, ATTN_BLOCK, LANES), F32),
            pltpu.VMEM((2 * SB_PAIRS_PER_STEP, ATTN_BLOCK, LANES), F32),
            pltpu.VMEM((2 * SB_PAIRS_PER_STEP, ATTN_BLOCK, LANES), BF16)],
        compiler_params=pltpu.CompilerParams(
            dimension_semantics=("parallel", "parallel"), vmem_limit_bytes=VMEM_LIMIT),
        name="sb_attention",
    )(qkv3, qkv3, qkv3)
    return out.reshape(batch * seq_len, d)


def _diff_attn_kernel(lam_ref, q_ref, k_ref, vt_ref, g_ref, o_ref, s_a, s_b, m_ref, l_ref, acc_ref, qs_ref, *,
                      n_heads, n_pad, lam_init, blk):
    lax.fori_loop(0, q_ref.shape[1] // blk, functools.partial(
        _diff_query_block, lam_ref, q_ref, k_ref, vt_ref, g_ref, o_ref, s_a, s_b, m_ref, l_ref, acc_ref, qs_ref,
        n_heads, n_pad, lam_init, blk), 0)


def _diff_query_block(lam_ref, q_ref, k_ref, vt_ref, g_ref, o_ref, s_a, s_b, m_ref, l_ref, acc_ref, qs_ref,
                      n_heads, n_pad, lam_init, blk, qi, _):
    units = range(2 * n_heads)
    q_start = pl.multiple_of(qi * blk, blk)
    lo_half = lax.broadcasted_iota(jnp.int32, (1, LANES), 1) < HEAD_DIM
    for h in range(n_heads):
        q = q_ref[0, pl.ds(q_start, blk), h * LANES:(h + 1) * LANES]
        zero = jnp.zeros_like(q)
        qs_ref[2 * h] = jnp.where(lo_half, q, zero)
        qs_ref[2 * h + 1] = jnp.where(lo_half, zero, q)
    key = lax.broadcasted_iota(jnp.int32, (blk, blk), 0)
    qry = lax.broadcasted_iota(jnp.int32, (blk, blk), 1)
    causal = key <= qry
    real_key = key >= n_pad if n_pad else None
    only = _all_of(causal, jnp.logical_or(real_key, qry < n_pad)) if n_pad else causal

    def scores_unit(t, s_ref, mask, u):
        h = u // 2
        start = pl.multiple_of(t * blk, blk)
        kb = k_ref[0, pl.ds(start, blk), h * LANES:(h + 1) * LANES]
        sc = lax.dot_general(kb, qs_ref[u], _NT, preferred_element_type=F32)
        s_ref[u] = sc if mask is None else jnp.where(mask, sc, -jnp.inf)

    def scores(t, s_ref, mask):
        for u in units:
            scores_unit(t, s_ref, mask, u)

    def softmax_unit(s_ref, u):
        sh = s_ref[u]
        m = m_ref[u]
        m_new = jnp.maximum(m, jnp.max(sh, axis=0, keepdims=True))
        alpha = jnp.exp2(m - m_new)
        ph = jnp.exp2(sh - m_new)
        l_ref[u] = alpha * l_ref[u] + jnp.sum(ph, axis=0, keepdims=True)
        m_ref[u] = m_new
        return alpha, ph.astype(BF16)

    def accumulate(t, u, alpha, ph):
        h = u // 2
        acc_ref[u] = alpha * acc_ref[u] + _dot(vt_ref[0, t, h * LANES:(h + 1) * LANES, :], ph)

    def absorb(t, s_ref):
        for u in units:
            accumulate(t, u, *softmax_unit(s_ref, u))

    def scores_and_absorb(t_next, next_ref, mask, t, s_ref):
        scores_unit(t_next, next_ref, mask, 0)
        for u in units:
            alpha, ph = softmax_unit(s_ref, u)
            if u + 1 < 2 * n_heads:
                scores_unit(t_next, next_ref, mask, u + 1)
            accumulate(t, u, alpha, ph)

    m_ref[...] = jnp.full_like(m_ref, -jnp.inf)
    l_ref[...] = jnp.zeros_like(l_ref)
    acc_ref[...] = jnp.zeros_like(acc_ref)
    _visit_blocks(qi + 1, scores, absorb, scores_and_absorb, s_a, s_b, first=real_key, last=causal, only=only)

    lp = lam_ref[...]
    lam = (jnp.exp(jnp.sum(lp[0:1] * lp[1:2], axis=-1, keepdims=True))
           - jnp.exp(jnp.sum(lp[2:3] * lp[3:4], axis=-1, keepdims=True)) + lam_init)
    for h in range(n_heads):
        o = (acc_ref[2 * h] * (1.0 / l_ref[2 * h])
             - lam * (acc_ref[2 * h + 1] * (1.0 / l_ref[2 * h + 1])))
        ms = jnp.mean(o * o, axis=0, keepdims=True)
        o = o * lax.rsqrt(ms + EPS) * (g_ref[...] * (1.0 - lam_init))
        o_ref[0, pl.ds(q_start, blk), h * LANES:(h + 1) * LANES] = o.T.astype(o_ref.dtype)
    return 0


def _diff_attention(q, k, vt, lam_params, subln_g, lam_init, batch, seq_len, n_pad):
    d = DIFF_HEADS * LANES
    width = DIFF_HEADS_PER_STEP * LANES
    groups = d // width
    n_blk = seq_len // ATTN_BLOCK
    q3 = q.reshape(batch, seq_len, d)
    k3 = k.reshape(batch, seq_len, d)
    vt4 = vt.reshape(batch, n_blk, d, ATTN_BLOCK)
    out = pl.pallas_call(
        functools.partial(_diff_attn_kernel, n_heads=DIFF_HEADS_PER_STEP, n_pad=n_pad, lam_init=lam_init,
                          blk=ATTN_BLOCK),
        grid=(batch, groups),
        in_specs=[
            pl.BlockSpec(lam_params.shape, lambda b, g: (0, 0)),
            pl.BlockSpec((1, seq_len, width), lambda b, g: (b, 0, g)),
            pl.BlockSpec((1, seq_len, width), lambda b, g: (b, 0, g)),
            pl.BlockSpec((1, n_blk, width, ATTN_BLOCK), lambda b, g: (b, 0, g, 0)),
            pl.BlockSpec((LANES, 1), lambda b, g: (0, 0)),
        ],
        out_specs=pl.BlockSpec((1, seq_len, width), lambda b, g: (b, 0, g)),
        out_shape=jax.ShapeDtypeStruct((batch, seq_len, d), BF16),
        scratch_shapes=[pltpu.VMEM((2 * DIFF_HEADS_PER_STEP, ATTN_BLOCK, ATTN_BLOCK), F32)] * 2 + [
            pltpu.VMEM((2 * DIFF_HEADS_PER_STEP, 1, ATTN_BLOCK), F32)] * 2 + [
            pltpu.VMEM((2 * DIFF_HEADS_PER_STEP, LANES, ATTN_BLOCK), F32),
            pltpu.VMEM((2 * DIFF_HEADS_PER_STEP, ATTN_BLOCK, LANES), BF16)],
        compiler_params=pltpu.CompilerParams(
            dimension_semantics=("parallel", "parallel"), vmem_limit_bytes=VMEM_LIMIT),
        name="diff_attention",
    )(lam_params, q3, k3, vt4, subln_g.reshape(LANES, 1))
    return out.reshape(batch * seq_len, d)


def _post_attn_kernel(h_ref, o_ref, wo_ref, g_ref, win_ref, wout_ref, out_ref):
    h1 = h_ref[...] + _dot(o_ref[...], wo_ref[...])
    hn = (h1 * _rms_scale(h1) * g_ref[...]).astype(BF16)
    acc = h1
    d_ff = win_ref.shape[1]
    for c0 in range(0, d_ff, FF_CHUNK):
        a = jnp.maximum(_dot(hn, win_ref[:, c0:c0 + FF_CHUNK]), 0.0)
        acc = acc + _dot((a * a).astype(BF16), wout_ref[c0:c0 + FF_CHUNK, :])
    out_ref[...] = acc


def _post_attention(h, o, wo, g, w_in, w_out, *, tm):
    n, d = h.shape
    row = pl.BlockSpec((tm, d), lambda i: (i, 0))
    return pl.pallas_call(
        _post_attn_kernel,
        grid=(n // tm,),
        in_specs=[row, pl.BlockSpec((tm, o.shape[1]), lambda i: (i, 0)), _resident(wo.shape),
                  _resident((1, d)), _resident(w_in.shape), _resident(w_out.shape)],
        out_specs=row,
        out_shape=jax.ShapeDtypeStruct((n, d), F32),
        compiler_params=pltpu.CompilerParams(
            dimension_semantics=("parallel",), vmem_limit_bytes=VMEM_LIMIT),
        name="post_attention",
    )(h, o, wo, g.reshape(1, d), w_in, w_out)


def _rope_tables(seq_len, n_pad):
    half = ROPE_DIM // 2
    inv_freq = jnp.power(ROPE_THETA, -jnp.arange(0, ROPE_DIM, 2, dtype=F32) / ROPE_DIM)
    pos = jnp.maximum(jnp.arange(seq_len) - n_pad, 0).astype(F32)
    ang = pos[:, None] * inv_freq[None, :]
    cos, sin = jnp.cos(ang), jnp.sin(ang)
    seg_lane = jnp.arange(LANES) % HEAD_DIM
    idx = seg_lane % half
    cos_t = jnp.where(seg_lane < ROPE_DIM, cos[:, idx], 1.0)
    s1_t = jnp.where(seg_lane < half, -sin[:, idx], 0.0)
    s2_t = jnp.where((seg_lane >= half) & (seg_lane < ROPE_DIM), sin[:, idx], 0.0)
    return cos_t, s1_t, s2_t


def kernel(x, meta_tokens, a_norm_g, a_w_qkv, a_w_o, kv_norm_g, kv_w, kv_k_norm_g, b_norm_g, b_w_q,
           b_q_norm_g, b_lambda, b_subln_g, b_w_o, mlp_norm_g, mlp_w_in, mlp_w_out):
    batch, s_len, d = x.shape
    n_a = a_w_qkv.shape[0]
    n_b = b_w_q.shape[0]
    real = s_len + N_META
    seq_len = -(-real // ATTN_BLOCK) * ATTN_BLOCK
    n_pad = seq_len - real
    n_rows = batch * seq_len
    tm = ROW_TILE if n_rows % ROW_TILE == 0 else ATTN_BLOCK

    meta = jnp.broadcast_to(meta_tokens.astype(x.dtype)[None], (batch, N_META, d))
    h = jnp.concatenate([jnp.zeros((batch, n_pad, d), x.dtype), meta, x], axis=1)
    h = h.reshape(n_rows, d)
    rope = _rope_tables(seq_len, n_pad)
    k_cols = DIFF_HEADS * 2 * HEAD_DIM
    qk_scale = HEAD_DIM ** -0.5 * math.log2(math.e)

    kv = None
    for layer in range(n_a + n_b):
        if layer < n_a:
            i = layer
            col_scale = jnp.concatenate([jnp.full((d,), qk_scale, F32), jnp.ones((2 * d,), F32)])
            w_qkv = (a_w_qkv[i] * col_scale).astype(BF16)
            qkv = _norm_proj(h, a_norm_g[i], w_qkv, tm=tm)
            o = _sb_attention(qkv, batch, seq_len, n_pad)
            w_o = a_w_o[i]
        else:
            j = layer - n_a
            if kv is None:
                k_gain = jnp.tile(kv_k_norm_g.reshape(2 * HEAD_DIM), DIFF_HEADS)
                kv = _norm_proj(h, kv_norm_g, kv_w[:, :k_cols].astype(BF16), tm=ATTN_BLOCK, seg_gain=k_gain,
                                rope=rope, n_rope_cols=k_cols, seq_len=seq_len,
                                w_t=kv_w[:, k_cols:].T.astype(BF16))
            lam_init = 0.8 - 0.6 * math.exp(-0.3 * layer)
            q_gain = jnp.tile(b_q_norm_g[j].reshape(2 * HEAD_DIM), DIFF_HEADS) * qk_scale
            q = _norm_proj(h, b_norm_g[j], b_w_q[j].astype(BF16), tm=ATTN_BLOCK, seg_gain=q_gain,
                           rope=rope, n_rope_cols=k_cols, seq_len=seq_len)
            o = _diff_attention(q, kv[0], kv[1], b_lambda[j].astype(F32), b_subln_g[j], lam_init, batch,
                                seq_len, n_pad)
            w_o = b_w_o[j]
        h = _post_attention(h, o, w_o.astype(BF16), mlp_norm_g[layer], mlp_w_in[layer].astype(BF16),
                            mlp_w_out[layer].astype(BF16), tm=tm)

    return h.reshape(batch, seq_len, d)[:, n_pad + N_META:]
```

```python
import functools
import math

import jax
import jax.numpy as jnp
from jax import lax
from jax.experimental import pallas as pl
from jax.experimental.pallas import tpu as pltpu

N_META = 16
SB_HEADS = 16
HEAD_DIM = 64
DIFF_HEADS = 8
ROPE_DIM = 16
ROPE_THETA = 500000.0
EPS = 1e-6

LANES = 128
ATTN_BLOCK = 256
ROW_TILE = 512
FF_CHUNK = 1024
SB_PAIRS_PER_STEP = 4
DIFF_HEADS_PER_STEP = 4
SB_STEPS_PER_TRIP = 2
DIFF_STEPS_PER_TRIP = 4
VMEM_LIMIT = 56 * 1024 * 1024

F32 = jnp.float32
BF16 = jnp.bfloat16
_NT = (((1,), (1,)), ((), ()))


def _dot(a, b):
    return jnp.dot(a, b, preferred_element_type=F32)


def _rms_scale(x):
    return lax.rsqrt(jnp.mean(x * x, axis=-1, keepdims=True) + EPS)


def _resident(shape):
    nd = len(shape)
    return pl.BlockSpec(shape, lambda *_: (0,) * nd, pipeline_mode=pl.Buffered(1))


def _norm_proj_kernel(h_ref, g_ref, w_ref, *rest, n_rope_cols, transposed_extra):
    wt_ref = ot_ref = None
    if transposed_extra:
        wt_ref, *rest, ot_ref = rest
    if n_rope_cols:
        sg_ref, cos_ref, s1_ref, s2_ref, o_ref = rest
    else:
        (o_ref,) = rest
    x = h_ref[...]
    hn = (x * _rms_scale(x) * g_ref[...]).astype(BF16)
    n_out = o_ref.shape[1]
    if n_rope_cols:
        r = lax.broadcasted_iota(jnp.int32, (LANES, LANES), 0) // HEAD_DIM
        c = lax.broadcasted_iota(jnp.int32, (LANES, LANES), 1) // HEAD_DIM
        seg = jnp.where(r == c, 1.0, 0.0).astype(BF16)
        cos, s1, s2 = cos_ref[...], s1_ref[...], s2_ref[...]
    chunk = min(FF_CHUNK, n_out)
    for c0 in range(0, n_out, chunk):
        y = _dot(hn, w_ref[:, c0:c0 + chunk])
        if c0 >= n_rope_cols:
            o_ref[:, c0:c0 + chunk] = y.astype(o_ref.dtype)
            continue
        for l0 in range(0, chunk, LANES):
            yc = y[:, l0:l0 + LANES]
            ms = _dot((yc * yc).astype(BF16), seg) * (1.0 / HEAD_DIM)
            yn = yc * lax.rsqrt(ms + EPS) * sg_ref[:, c0 + l0:c0 + l0 + LANES]
            rot = (yn * cos
                   + pltpu.roll(yn, LANES - ROPE_DIM // 2, 1) * s1
                   + pltpu.roll(yn, ROPE_DIM // 2, 1) * s2)
            o_ref[:, c0 + l0:c0 + l0 + LANES] = rot.astype(o_ref.dtype)
    if transposed_extra:
        ot_ref[0] = lax.dot_general(wt_ref[...], hn, _NT, preferred_element_type=F32).astype(ot_ref.dtype)


def _norm_proj(h, g, w, *, tm, seg_gain=None, rope=None, n_rope_cols=0, seq_len=None, w_t=None):
    n, d = h.shape
    n_out = w.shape[1]
    in_specs = [pl.BlockSpec((tm, d), lambda i: (i, 0)), _resident((1, d)), _resident((d, n_out))]
    args = [h, g.reshape(1, d), w]
    if w_t is not None:
        in_specs.append(_resident(w_t.shape))
        args.append(w_t)
    if n_rope_cols:
        tiles_per_seq = seq_len // tm
        tab = pl.BlockSpec((tm, LANES), lambda i: (i % tiles_per_seq, 0))
        in_specs += [_resident((1, n_rope_cols)), tab, tab, tab]
        args += [seg_gain.reshape(1, n_rope_cols), *rope]
    out_specs = pl.BlockSpec((tm, n_out), lambda i: (i, 0))
    out_shape = jax.ShapeDtypeStruct((n, n_out), BF16)
    if w_t is not None:
        out_specs = [out_specs, pl.BlockSpec((1, w_t.shape[0], tm), lambda i: (i, 0, 0))]
        out_shape = [out_shape, jax.ShapeDtypeStruct((n // tm, w_t.shape[0], tm), BF16)]
    return pl.pallas_call(
        functools.partial(_norm_proj_kernel, n_rope_cols=n_rope_cols, transposed_extra=w_t is not None),
        grid=(n // tm,),
        in_specs=in_specs,
        out_specs=out_specs,
        out_shape=out_shape,
        compiler_params=pltpu.CompilerParams(
            dimension_semantics=("parallel",), vmem_limit_bytes=VMEM_LIMIT),
        name="norm_proj_rope" if n_rope_cols else "norm_proj",
    )(*args)


def _stack_halves(q, lo_half):
    zero = jnp.zeros_like(q)
    return jnp.concatenate([jnp.where(lo_half, q, zero), jnp.where(lo_half, zero, q)], axis=0)


def _neg_abs(x):
    sign = jnp.uint32(0x80000000)
    return lax.bitcast_convert_type(lax.bitcast_convert_type(x, jnp.uint32) | sign, F32)


def _all_of(*masks):
    masks = [m for m in masks if m is not None]
    return functools.reduce(jnp.logical_and, masks) if masks else None


def _visit_blocks(n_steps, produce, consume, produce_and_consume, buf_a, buf_b, first, last, only, unroll):
    assert unroll % 2 == 0
    bufs = (buf_a, buf_b)

    @pl.when(n_steps == 1)
    def _():
        produce(0, buf_a, only)
        consume(0, buf_a)

    @pl.when(n_steps > 1)
    def _():
        produce(0, buf_a, first)

        def trip(k, _):
            t = unroll * k
            for i in range(unroll):
                produce_and_consume(t + i + 1, bufs[(i + 1) % 2], None, t + i, bufs[i % 2])
            return 0

        n_trips = (n_steps - 2) // unroll
        lax.fori_loop(0, n_trips, trip, 0)
        base = unroll * n_trips
        left = n_steps - 1 - base
        for m in range(1, unroll + 1):
            @pl.when(left == m)
            def _(m=m):
                for i in range(1, m + 1):
                    produce_and_consume(base + i, bufs[i % 2], last if i == m else None,
                                        base + i - 1, bufs[(i - 1) % 2])
                consume(base + m, bufs[m % 2])


def _sb_attn_kernel(q_ref, k_ref, v_ref, o_ref, y_a, y_b, carry_ref, acc_ref, qs_ref, *, n_pairs, n_pad, blk):
    lax.fori_loop(0, q_ref.shape[1] // blk, functools.partial(
        _sb_query_block, q_ref, k_ref, v_ref, o_ref, y_a, y_b, carry_ref, acc_ref, qs_ref, n_pairs, n_pad, blk), 0)


def _sb_query_block(q_ref, k_ref, v_ref, o_ref, y_a, y_b, carry_ref, acc_ref, qs_ref, n_pairs, n_pad, blk, qi, _):
    pairs = range(n_pairs)
    q_start = pl.multiple_of(qi * blk, blk)
    lo_half = lax.broadcasted_iota(jnp.int32, (1, LANES), 1) < HEAD_DIM
    for p in pairs:
        qs_ref[p] = _stack_halves(q_ref[0, pl.ds(q_start, blk), p * LANES:(p + 1) * LANES], lo_half)
    qs = [qs_ref.at[p] for p in pairs]
    row = lax.broadcasted_iota(jnp.int32, (2 * blk, blk), 0)
    col = lax.broadcasted_iota(jnp.int32, (2 * blk, blk), 1)
    key_j = lax.broadcasted_iota(jnp.int32, (blk, blk), 0)
    key_s = lax.broadcasted_iota(jnp.int32, (blk, blk), 1)
    neg_tri = jnp.where(key_j >= key_s, -1.0, 0.0).astype(BF16)
    past = col < row % blk
    real_key = col >= n_pad if n_pad else None

    def logits_pair(t, y_ref, mask, p):
        start = pl.multiple_of((qi - t) * blk, blk)
        kb = k_ref[0, pl.ds(start, blk), p * LANES:(p + 1) * LANES]
        y = lax.dot_general(qs[p][...], kb, _NT, preferred_element_type=F32)
        y_ref[p] = y if mask is None else jnp.where(mask, y, -jnp.inf)

    def logits(t, y_ref, mask):
        for p in pairs:
            logits_pair(t, y_ref, mask, p)

    def cum_pair(y_ref, p):
        y = y_ref[p]
        sp = jnp.maximum(y, 0.0) + jnp.log2(1.0 + jnp.exp2(_neg_abs(y)))
        return _dot(sp.astype(BF16), neg_tri)

    def weigh_pair(t, y_ref, p, cum):
        start = pl.multiple_of((qi - t) * blk, blk)
        vb = v_ref[0, pl.ds(start, blk), p * LANES:(p + 1) * LANES]
        carry = carry_ref[p]
        w = jnp.exp2((y_ref[p] + cum + jnp.concatenate([carry, carry], axis=1)).astype(BF16))
        acc_ref[p, 0] += _dot(w[:blk], vb)
        acc_ref[p, 1] += _dot(w[blk:], vb)
        carry_ref[p] = carry + jnp.broadcast_to(cum[:, 0:1], carry.shape)

    def absorb(t, y_ref):
        cum = [cum_pair(y_ref, p) for p in pairs]
        for p in pairs:
            weigh_pair(t, y_ref, p, cum[p])

    def logits_and_absorb(t_next, next_ref, mask, t, y_ref):
        cum = {}
        for p in pairs:
            logits_pair(t_next, next_ref, mask, p)
            cum[p] = cum_pair(y_ref, p)
            if p > 0:
                weigh_pair(t, y_ref, p - 1, cum[p - 1])
        weigh_pair(t, y_ref, n_pairs - 1, cum[n_pairs - 1])

    carry_ref[...] = jnp.zeros_like(carry_ref)
    acc_ref[...] = jnp.zeros_like(acc_ref)
    _visit_blocks(qi + 1, logits, absorb, logits_and_absorb, y_a, y_b,
                  first=past, last=real_key, only=_all_of(past, real_key), unroll=SB_STEPS_PER_TRIP)
    for p in pairs:
        o_ref[0, pl.ds(q_start, blk), p * LANES:(p + 1) * LANES] = jnp.where(
            lo_half, acc_ref[p, 0], acc_ref[p, 1]).astype(o_ref.dtype)
    return 0


def _sb_attention(qkv, batch, seq_len, n_pad):
    d = SB_HEADS * HEAD_DIM
    width = SB_PAIRS_PER_STEP * LANES
    groups = d // width
    qkv3 = qkv.reshape(batch, seq_len, 3 * d)
    out = pl.pallas_call(
        functools.partial(_sb_attn_kernel, n_pairs=SB_PAIRS_PER_STEP, n_pad=n_pad, blk=ATTN_BLOCK),
        grid=(batch, groups),
        in_specs=[
            pl.BlockSpec((1, seq_len, width), lambda b, g: (b, 0, g)),
            pl.BlockSpec((1, seq_len, width), lambda b, g: (b, 0, groups + g)),
            pl.BlockSpec((1, seq_len, width), lambda b, g: (b, 0, 2 * groups + g)),
        ],
        out_specs=pl.BlockSpec((1, seq_len, width), lambda b, g: (b, 0, g)),
        out_shape=jax.ShapeDtypeStruct((batch, seq_len, d), BF16),
        scratch_shapes=[pltpu.VMEM((SB_PAIRS_PER_STEP, 2 * ATTN_BLOCK, ATTN_BLOCK), F32)] * 2 + [
            pltpu.VMEM((SB_PAIRS_PER_STEP, 2 * ATTN_BLOCK, LANES), F32),
            pltpu.VMEM((SB_PAIRS_PER_STEP, 2, ATTN_BLOCK, LANES), F32),
            pltpu.VMEM((SB_PAIRS_PER_STEP, 2 * ATTN_BLOCK, LANES), BF16)],
        compiler_params=pltpu.CompilerParams(
            dimension_semantics=("parallel", "parallel"), vmem_limit_bytes=VMEM_LIMIT),
        name="sb_attention",
    )(qkv3, qkv3, qkv3)
    return out.reshape(batch * seq_len, d)


def _diff_attn_kernel(lam_ref, q_ref, k_ref, vt_ref, g_ref, o_ref, s_a, s_b, m_ref, l_ref, acc_ref, qs_ref, *,
                      n_heads, n_pad, lam_init, blk):
    lax.fori_loop(0, q_ref.shape[1] // blk, functools.partial(
        _diff_query_block, lam_ref, q_ref, k_ref, vt_ref, g_ref, o_ref, s_a, s_b, m_ref, l_ref, acc_ref, qs_ref,
        n_heads, n_pad, lam_init, blk), 0)


def _diff_query_block(lam_ref, q_ref, k_ref, vt_ref, g_ref, o_ref, s_a, s_b, m_ref, l_ref, acc_ref, qs_ref,
                      n_heads, n_pad, lam_init, blk, qi, _):
    units = range(2 * n_heads)
    q_start = pl.multiple_of(qi * blk, blk)
    lo_half = lax.broadcasted_iota(jnp.int32, (1, LANES), 1) < HEAD_DIM
    for h in range(n_heads):
        q = q_ref[0, pl.ds(q_start, blk), h * LANES:(h + 1) * LANES]
        zero = jnp.zeros_like(q)
        qs_ref[2 * h] = jnp.where(lo_half, q, zero)
        qs_ref[2 * h + 1] = jnp.where(lo_half, zero, q)
    key = lax.broadcasted_iota(jnp.int32, (blk, blk), 0)
    qry = lax.broadcasted_iota(jnp.int32, (blk, blk), 1)
    causal = key <= qry
    real_key = key >= n_pad if n_pad else None
    only = _all_of(causal, jnp.logical_or(real_key, qry < n_pad)) if n_pad else causal

    def scores_unit(t, s_ref, mask, u):
        h = u // 2
        start = pl.multiple_of(t * blk, blk)
        kb = k_ref[0, pl.ds(start, blk), h * LANES:(h + 1) * LANES]
        sc = lax.dot_general(kb, qs_ref[u], _NT, preferred_element_type=F32)
        s_ref[u] = sc if mask is None else jnp.where(mask, sc, -jnp.inf)

    def scores(t, s_ref, mask):
        for u in units:
            scores_unit(t, s_ref, mask, u)

    def softmax_unit(s_ref, u):
        sh = s_ref[u]
        m = m_ref[u]
        m_new = jnp.maximum(m, jnp.max(sh, axis=0, keepdims=True))
        alpha = jnp.exp2(m - m_new)
        ph = jnp.exp2(sh - m_new)
        l_ref[u] = alpha * l_ref[u] + jnp.sum(ph, axis=0, keepdims=True)
        m_ref[u] = m_new
        return alpha, ph.astype(BF16)

    def accumulate(t, u, alpha, ph):
        h = u // 2
        acc_ref[u] = alpha * acc_ref[u] + _dot(vt_ref[0, t, h * LANES:(h + 1) * LANES, :], ph)

    def absorb(t, s_ref):
        for u in units:
            accumulate(t, u, *softmax_unit(s_ref, u))

    def scores_and_absorb(t_next, next_ref, mask, t, s_ref):
        scores_unit(t_next, next_ref, mask, 0)
        for u in units:
            alpha, ph = softmax_unit(s_ref, u)
            if u + 1 < 2 * n_heads:
                scores_unit(t_next, next_ref, mask, u + 1)
            accumulate(t, u, alpha, ph)

    m_ref[...] = jnp.full_like(m_ref, -jnp.inf)
    l_ref[...] = jnp.zeros_like(l_ref)
    acc_ref[...] = jnp.zeros_like(acc_ref)
    _visit_blocks(qi + 1, scores, absorb, scores_and_absorb, s_a, s_b, first=real_key, last=causal, only=only,
                  unroll=DIFF_STEPS_PER_TRIP)

    lp = lam_ref[...]
    lam = (jnp.exp(jnp.sum(lp[0:1] * lp[1:2], axis=-1, keepdims=True))
           - jnp.exp(jnp.sum(lp[2:3] * lp[3:4], axis=-1, keepdims=True)) + lam_init)
    for h in range(n_heads):
        o = (acc_ref[2 * h] * (1.0 / l_ref[2 * h])
             - lam * (acc_ref[2 * h + 1] * (1.0 / l_ref[2 * h + 1])))
        ms = jnp.mean(o * o, axis=0, keepdims=True)
        o = o * lax.rsqrt(ms + EPS) * (g_ref[...] * (1.0 - lam_init))
        o_ref[0, pl.ds(q_start, blk), h * LANES:(h + 1) * LANES] = o.T.astype(o_ref.dtype)
    return 0


def _diff_attention(q, k, vt, lam_params, subln_g, lam_init, batch, seq_len, n_pad):
    d = DIFF_HEADS * LANES
    width = DIFF_HEADS_PER_STEP * LANES
    groups = d // width
    n_blk = seq_len // ATTN_BLOCK
    q3 = q.reshape(batch, seq_len, d)
    k3 = k.reshape(batch, seq_len, d)
    vt4 = vt.reshape(batch, n_blk, d, ATTN_BLOCK)
    out = pl.pallas_call(
        functools.partial(_diff_attn_kernel, n_heads=DIFF_HEADS_PER_STEP, n_pad=n_pad, lam_init=lam_init,
                          blk=ATTN_BLOCK),
        grid=(batch, groups),
        in_specs=[
            pl.BlockSpec(lam_params.shape, lambda b, g: (0, 0)),
            pl.BlockSpec((1, seq_len, width), lambda b, g: (b, 0, g)),
            pl.BlockSpec((1, seq_len, width), lambda b, g: (b, 0, g)),
            pl.BlockSpec((1, n_blk, width, ATTN_BLOCK), lambda b, g: (b, 0, g, 0)),
            pl.BlockSpec((LANES, 1), lambda b, g: (0, 0)),
        ],
        out_specs=pl.BlockSpec((1, seq_len, width), lambda b, g: (b, 0, g)),
        out_shape=jax.ShapeDtypeStruct((batch, seq_len, d), BF16),
        scratch_shapes=[pltpu.VMEM((2 * DIFF_HEADS_PER_STEP, ATTN_BLOCK, ATTN_BLOCK), F32)] * 2 + [
            pltpu.VMEM((2 * DIFF_HEADS_PER_STEP, 1, ATTN_BLOCK), F32)] * 2 + [
            pltpu.VMEM((2 * DIFF_HEADS_PER_STEP, LANES, ATTN_BLOCK), F32),
            pltpu.VMEM((2 * DIFF_HEADS_PER_STEP, ATTN_BLOCK, LANES), BF16)],
        compiler_params=pltpu.CompilerParams(
            dimension_semantics=("parallel", "parallel"), vmem_limit_bytes=VMEM_LIMIT),
        name="diff_attention",
    )(lam_params, q3, k3, vt4, subln_g.reshape(LANES, 1))
    return out.reshape(batch * seq_len, d)


def _post_attn_kernel(h_ref, o_ref, wo_ref, g_ref, win_ref, wout_ref, out_ref):
    h1 = h_ref[...] + _dot(o_ref[...], wo_ref[...])
    hn = (h1 * _rms_scale(h1) * g_ref[...]).astype(BF16)
    acc = h1
    d_ff = win_ref.shape[1]
    for c0 in range(0, d_ff, FF_CHUNK):
        a = jnp.maximum(_dot(hn, win_ref[:, c0:c0 + FF_CHUNK]), 0.0)
        acc = acc + _dot((a * a).astype(BF16), wout_ref[c0:c0 + FF_CHUNK, :])
    out_ref[...] = acc


def _post_attention(h, o, wo, g, w_in, w_out, *, tm):
    n, d = h.shape
    row = pl.BlockSpec((tm, d), lambda i: (i, 0))
    return pl.pallas_call(
        _post_attn_kernel,
        grid=(n // tm,),
        in_specs=[row, pl.BlockSpec((tm, o.shape[1]), lambda i: (i, 0)), _resident(wo.shape),
                  _resident((1, d)), _resident(w_in.shape), _resident(w_out.shape)],
        out_specs=row,
        out_shape=jax.ShapeDtypeStruct((n, d), F32),
        compiler_params=pltpu.CompilerParams(
            dimension_semantics=("parallel",), vmem_limit_bytes=VMEM_LIMIT),
        name="post_attention",
    )(h, o, wo, g.reshape(1, d), w_in, w_out)


def _rope_tables(seq_len, n_pad):
    half = ROPE_DIM // 2
    inv_freq = jnp.power(ROPE_THETA, -jnp.arange(0, ROPE_DIM, 2, dtype=F32) / ROPE_DIM)
    pos = jnp.maximum(jnp.arange(seq_len) - n_pad, 0).astype(F32)
    ang = pos[:, None] * inv_freq[None, :]
    cos, sin = jnp.cos(ang), jnp.sin(ang)
    seg_lane = jnp.arange(LANES) % HEAD_DIM
    idx = seg_lane % half
    cos_t = jnp.where(seg_lane < ROPE_DIM, cos[:, idx], 1.0)
    s1_t = jnp.where(seg_lane < half, -sin[:, idx], 0.0)
    s2_t = jnp.where((seg_lane >= half) & (seg_lane < ROPE_DIM), sin[:, idx], 0.0)
    return cos_t, s1_t, s2_t


def kernel(x, meta_tokens, a_norm_g, a_w_qkv, a_w_o, kv_norm_g, kv_w, kv_k_norm_g, b_norm_g, b_w_q,
           b_q_norm_g, b_lambda, b_subln_g, b_w_o, mlp_norm_g, mlp_w_in, mlp_w_out):
    batch, s_len, d = x.shape
    n_a = a_w_qkv.shape[0]
    n_b = b_w_q.shape[0]
    real = s_len + N_META
    seq_len = -(-real // ATTN_BLOCK) * ATTN_BLOCK
    n_pad = seq_len - real
    n_rows = batch * seq_len
    tm = ROW_TILE if n_rows % ROW_TILE == 0 else ATTN_BLOCK

    meta = jnp.broadcast_to(meta_tokens.astype(x.dtype)[None], (batch, N_META, d))
    h = jnp.concatenate([jnp.zeros((batch, n_pad, d), x.dtype), meta, x], axis=1)
    h = h.reshape(n_rows, d)
    rope = _rope_tables(seq_len, n_pad)
    k_cols = DIFF_HEADS * 2 * HEAD_DIM
    qk_scale = HEAD_DIM ** -0.5 * math.log2(math.e)

    kv = None
    for layer in range(n_a + n_b):
        if layer < n_a:
            i = layer
            col_scale = jnp.concatenate([jnp.full((d,), qk_scale, F32), jnp.ones((2 * d,), F32)])
            w_qkv = (a_w_qkv[i] * col_scale).astype(BF16)
            qkv = _norm_proj(h, a_norm_g[i], w_qkv, tm=tm)
            o = _sb_attention(qkv, batch, seq_len, n_pad)
            w_o = a_w_o[i]
        else:
            j = layer - n_a
            if kv is None:
                k_gain = jnp.tile(kv_k_norm_g.reshape(2 * HEAD_DIM), DIFF_HEADS)
                kv = _norm_proj(h, kv_norm_g, kv_w[:, :k_cols].astype(BF16), tm=ATTN_BLOCK, seg_gain=k_gain,
                                rope=rope, n_rope_cols=k_cols, seq_len=seq_len,
                                w_t=kv_w[:, k_cols:].T.astype(BF16))
            lam_init = 0.8 - 0.6 * math.exp(-0.3 * layer)
            q_gain = jnp.tile(b_q_norm_g[j].reshape(2 * HEAD_DIM), DIFF_HEADS) * qk_scale
            q = _norm_proj(h, b_norm_g[j], b_w_q[j].astype(BF16), tm=ATTN_BLOCK, seg_gain=q_gain,
                           rope=rope, n_rope_cols=k_cols, seq_len=seq_len)
            o = _diff_attention(q, kv[0], kv[1], b_lambda[j].astype(F32), b_subln_g[j], lam_init, batch,
                                seq_len, n_pad)
            w_o = b_w_o[j]
        h = _post_attention(h, o, w_o.astype(BF16), mlp_norm_g[layer], mlp_w_in[layer].astype(BF16),
                            mlp_w_out[layer].astype(BF16), tm=tm)

    return h.reshape(batch, seq_len, d)[:, n_pad + N_META:]
```

```python
import functools
import math

import jax
import jax.numpy as jnp
from jax import lax
from jax.experimental import pallas as pl
from jax.experimental.pallas import tpu as pltpu

N_META = 16
SB_HEADS = 16
HEAD_DIM = 64
DIFF_HEADS = 8
ROPE_DIM = 16
ROPE_THETA = 500000.0
EPS = 1e-6

LANES = 128
ATTN_BLOCK = 256
ROW_TILE = 512
FF_CHUNK = 1024
SB_PAIRS_PER_STEP = 4
DIFF_HEADS_PER_STEP = 4
SB_STEPS_PER_TRIP = 4
DIFF_STEPS_PER_TRIP = 6
VMEM_LIMIT = 56 * 1024 * 1024

F32 = jnp.float32
BF16 = jnp.bfloat16
_NT = (((1,), (1,)), ((), ()))


def _dot(a, b):
    return jnp.dot(a, b, preferred_element_type=F32)


def _rms_scale(x):
    return lax.rsqrt(jnp.mean(x * x, axis=-1, keepdims=True) + EPS)


def _resident(shape):
    nd = len(shape)
    return pl.BlockSpec(shape, lambda *_: (0,) * nd, pipeline_mode=pl.Buffered(1))


def _norm_proj_kernel(h_ref, g_ref, w_ref, *rest, n_rope_cols, transposed_extra):
    wt_ref = ot_ref = None
    if transposed_extra:
        wt_ref, *rest, ot_ref = rest
    if n_rope_cols:
        sg_ref, cos_ref, s1_ref, s2_ref, o_ref = rest
    else:
        (o_ref,) = rest
    x = h_ref[...]
    hn = (x * _rms_scale(x) * g_ref[...]).astype(BF16)
    n_out = o_ref.shape[1]
    if n_rope_cols:
        r = lax.broadcasted_iota(jnp.int32, (LANES, LANES), 0) // HEAD_DIM
        c = lax.broadcasted_iota(jnp.int32, (LANES, LANES), 1) // HEAD_DIM
        seg = jnp.where(r == c, 1.0, 0.0).astype(BF16)
        cos, s1, s2 = cos_ref[...], s1_ref[...], s2_ref[...]
    chunk = min(FF_CHUNK, n_out)
    for c0 in range(0, n_out, chunk):
        y = _dot(hn, w_ref[:, c0:c0 + chunk])
        if c0 >= n_rope_cols:
            o_ref[:, c0:c0 + chunk] = y.astype(o_ref.dtype)
            continue
        for l0 in range(0, chunk, LANES):
            yc = y[:, l0:l0 + LANES]
            ms = _dot((yc * yc).astype(BF16), seg) * (1.0 / HEAD_DIM)
            yn = yc * lax.rsqrt(ms + EPS) * sg_ref[:, c0 + l0:c0 + l0 + LANES]
            rot = (yn * cos
                   + pltpu.roll(yn, LANES - ROPE_DIM // 2, 1) * s1
                   + pltpu.roll(yn, ROPE_DIM // 2, 1) * s2)
            o_ref[:, c0 + l0:c0 + l0 + LANES] = rot.astype(o_ref.dtype)
    if transposed_extra:
        ot_ref[0] = lax.dot_general(wt_ref[...], hn, _NT, preferred_element_type=F32).astype(ot_ref.dtype)


def _norm_proj(h, g, w, *, tm, seg_gain=None, rope=None, n_rope_cols=0, seq_len=None, w_t=None):
    n, d = h.shape
    n_out = w.shape[1]
    in_specs = [pl.BlockSpec((tm, d), lambda i: (i, 0)), _resident((1, d)), _resident((d, n_out))]
    args = [h, g.reshape(1, d), w]
    if w_t is not None:
        in_specs.append(_resident(w_t.shape))
        args.append(w_t)
    if n_rope_cols:
        tiles_per_seq = seq_len // tm
        tab = pl.BlockSpec((tm, LANES), lambda i: (i % tiles_per_seq, 0))
        in_specs += [_resident((1, n_rope_cols)), tab, tab, tab]
        args += [seg_gain.reshape(1, n_rope_cols), *rope]
    out_specs = pl.BlockSpec((tm, n_out), lambda i: (i, 0))
    out_shape = jax.ShapeDtypeStruct((n, n_out), BF16)
    if w_t is not None:
        out_specs = [out_specs, pl.BlockSpec((1, w_t.shape[0], tm), lambda i: (i, 0, 0))]
        out_shape = [out_shape, jax.ShapeDtypeStruct((n // tm, w_t.shape[0], tm), BF16)]
    return pl.pallas_call(
        functools.partial(_norm_proj_kernel, n_rope_cols=n_rope_cols, transposed_extra=w_t is not None),
        grid=(n // tm,),
        in_specs=in_specs,
        out_specs=out_specs,
        out_shape=out_shape,
        compiler_params=pltpu.CompilerParams(
            dimension_semantics=("parallel",), vmem_limit_bytes=VMEM_LIMIT),
        name="norm_proj_rope" if n_rope_cols else "norm_proj",
    )(*args)


def _stack_halves(q, lo_half):
    zero = jnp.zeros_like(q)
    return jnp.concatenate([jnp.where(lo_half, q, zero), jnp.where(lo_half, zero, q)], axis=0)


def _neg_abs(x):
    sign = jnp.uint32(0x80000000)
    return lax.bitcast_convert_type(lax.bitcast_convert_type(x, jnp.uint32) | sign, F32)


def _all_of(*masks):
    masks = [m for m in masks if m is not None]
    return functools.reduce(jnp.logical_and, masks) if masks else None


def _visit_blocks(n_steps, produce, consume, produce_and_consume, buf_a, buf_b, first, last, only, unroll):
    assert unroll % 2 == 0
    bufs = (buf_a, buf_b)

    @pl.when(n_steps == 1)
    def _():
        produce(0, buf_a, only)
        consume(0, buf_a)

    @pl.when(n_steps > 1)
    def _():
        produce(0, buf_a, first)

        def trip(k, _):
            t = unroll * k
            for i in range(unroll):
                produce_and_consume(t + i + 1, bufs[(i + 1) % 2], None, t + i, bufs[i % 2])
            return 0

        n_trips = (n_steps - 2) // unroll
        lax.fori_loop(0, n_trips, trip, 0)
        base = unroll * n_trips
        left = n_steps - 1 - base
        for m in range(1, unroll + 1):
            @pl.when(left == m)
            def _(m=m):
                for i in range(1, m + 1):
                    produce_and_consume(base + i, bufs[i % 2], last if i == m else None,
                                        base + i - 1, bufs[(i - 1) % 2])
                consume(base + m, bufs[m % 2])


def _sb_attn_kernel(q_ref, k_ref, v_ref, o_ref, y_a, y_b, carry_ref, acc_ref, qs_ref, *, n_pairs, n_pad, blk):
    lax.fori_loop(0, q_ref.shape[1] // blk, functools.partial(
        _sb_query_block, q_ref, k_ref, v_ref, o_ref, y_a, y_b, carry_ref, acc_ref, qs_ref, n_pairs, n_pad, blk), 0)


def _sb_query_block(q_ref, k_ref, v_ref, o_ref, y_a, y_b, carry_ref, acc_ref, qs_ref, n_pairs, n_pad, blk, qi, _):
    pairs = range(n_pairs)
    q_start = pl.multiple_of(qi * blk, blk)
    lo_half = lax.broadcasted_iota(jnp.int32, (1, LANES), 1) < HEAD_DIM
    for p in pairs:
        qs_ref[p] = _stack_halves(q_ref[0, pl.ds(q_start, blk), p * LANES:(p + 1) * LANES], lo_half)
    qs = [qs_ref.at[p] for p in pairs]
    row = lax.broadcasted_iota(jnp.int32, (2 * blk, blk), 0)
    col = lax.broadcasted_iota(jnp.int32, (2 * blk, blk), 1)
    key_j = lax.broadcasted_iota(jnp.int32, (blk, blk), 0)
    key_s = lax.broadcasted_iota(jnp.int32, (blk, blk), 1)
    neg_tri = jnp.where(key_j >= key_s, -1.0, 0.0).astype(BF16)
    past = col < row % blk
    real_key = col >= n_pad if n_pad else None

    def logits_pair(t, y_ref, mask, p):
        start = pl.multiple_of((qi - t) * blk, blk)
        kb = k_ref[0, pl.ds(start, blk), p * LANES:(p + 1) * LANES]
        y = lax.dot_general(qs[p][...], kb, _NT, preferred_element_type=F32)
        y_ref[p] = y if mask is None else jnp.where(mask, y, -jnp.inf)

    def logits(t, y_ref, mask):
        for p in pairs:
            logits_pair(t, y_ref, mask, p)

    def cum_pair(y_ref, p):
        y = y_ref[p]
        sp = jnp.maximum(y, 0.0) + jnp.log2(1.0 + jnp.exp2(_neg_abs(y)))
        return _dot(sp.astype(BF16), neg_tri)

    def weigh_pair(t, y_ref, p, cum):
        start = pl.multiple_of((qi - t) * blk, blk)
        vb = v_ref[0, pl.ds(start, blk), p * LANES:(p + 1) * LANES]
        carry = carry_ref[p]
        w = jnp.exp2((y_ref[p] + cum + jnp.concatenate([carry, carry], axis=1)).astype(BF16))
        acc_ref[p, 0] += _dot(w[:blk], vb)
        acc_ref[p, 1] += _dot(w[blk:], vb)
        carry_ref[p] = carry + jnp.broadcast_to(cum[:, 0:1], carry.shape)

    def absorb(t, y_ref):
        cum = [cum_pair(y_ref, p) for p in pairs]
        for p in pairs:
            weigh_pair(t, y_ref, p, cum[p])

    def logits_and_absorb(t_next, next_ref, mask, t, y_ref):
        cum = {}
        for p in pairs:
            logits_pair(t_next, next_ref, mask, p)
            cum[p] = cum_pair(y_ref, p)
            if p > 0:
                weigh_pair(t, y_ref, p - 1, cum[p - 1])
        weigh_pair(t, y_ref, n_pairs - 1, cum[n_pairs - 1])

    carry_ref[...] = jnp.zeros_like(carry_ref)
    acc_ref[...] = jnp.zeros_like(acc_ref)
    _visit_blocks(qi + 1, logits, absorb, logits_and_absorb, y_a, y_b,
                  first=past, last=real_key, only=_all_of(past, real_key), unroll=SB_STEPS_PER_TRIP)
    for p in pairs:
        o_ref[0, pl.ds(q_start, blk), p * LANES:(p + 1) * LANES] = jnp.where(
            lo_half, acc_ref[p, 0], acc_ref[p, 1]).astype(o_ref.dtype)
    return 0


def _sb_attention(qkv, batch, seq_len, n_pad):
    d = SB_HEADS * HEAD_DIM
    width = SB_PAIRS_PER_STEP * LANES
    groups = d // width
    qkv3 = qkv.reshape(batch, seq_len, 3 * d)
    out = pl.pallas_call(
        functools.partial(_sb_attn_kernel, n_pairs=SB_PAIRS_PER_STEP, n_pad=n_pad, blk=ATTN_BLOCK),
        grid=(batch, groups),
        in_specs=[
            pl.BlockSpec((1, seq_len, width), lambda b, g: (b, 0, g)),
            pl.BlockSpec((1, seq_len, width), lambda b, g: (b, 0, groups + g)),
            pl.BlockSpec((1, seq_len, width), lambda b, g: (b, 0, 2 * groups + g)),
        ],
        out_specs=pl.BlockSpec((1, seq_len, width), lambda b, g: (b, 0, g)),
        out_shape=jax.ShapeDtypeStruct((batch, seq_len, d), BF16),
        scratch_shapes=[pltpu.VMEM((SB_PAIRS_PER_STEP, 2 * ATTN_BLOCK, ATTN_BLOCK), F32)] * 2 + [
            pltpu.VMEM((SB_PAIRS_PER_STEP, 2 * ATTN_BLOCK, LANES), F32),
            pltpu.VMEM((SB_PAIRS_PER_STEP, 2, ATTN_BLOCK, LANES), F32),
            pltpu.VMEM((SB_PAIRS_PER_STEP, 2 * ATTN_BLOCK, LANES), BF16)],
        compiler_params=pltpu.CompilerParams(
            dimension_semantics=("parallel", "parallel"), vmem_limit_bytes=VMEM_LIMIT),
        name="sb_attention",
    )(qkv3, qkv3, qkv3)
    return out.reshape(batch * seq_len, d)


def _diff_attn_kernel(lam_ref, q_ref, k_ref, vt_ref, g_ref, o_ref, s_a, s_b, m_ref, l_ref, acc_ref, qs_ref, *,
                      n_heads, n_pad, lam_init, blk):
    lax.fori_loop(0, q_ref.shape[1] // blk, functools.partial(
        _diff_query_block, lam_ref, q_ref, k_ref, vt_ref, g_ref, o_ref, s_a, s_b, m_ref, l_ref, acc_ref, qs_ref,
        n_heads, n_pad, lam_init, blk), 0)


def _diff_query_block(lam_ref, q_ref, k_ref, vt_ref, g_ref, o_ref, s_a, s_b, m_ref, l_ref, acc_ref, qs_ref,
                      n_heads, n_pad, lam_init, blk, qi, _):
    units = range(2 * n_heads)
    q_start = pl.multiple_of(qi * blk, blk)
    lo_half = lax.broadcasted_iota(jnp.int32, (1, LANES), 1) < HEAD_DIM
    for h in range(n_heads):
        q = q_ref[0, pl.ds(q_start, blk), h * LANES:(h + 1) * LANES]
        zero = jnp.zeros_like(q)
        qs_ref[2 * h] = jnp.where(lo_half, q, zero)
        qs_ref[2 * h + 1] = jnp.where(lo_half, zero, q)
    key = lax.broadcasted_iota(jnp.int32, (blk, blk), 0)
    qry = lax.broadcasted_iota(jnp.int32, (blk, blk), 1)
    causal = key <= qry
    real_key = key >= n_pad if n_pad else None
    only = _all_of(causal, jnp.logical_or(real_key, qry < n_pad)) if n_pad else causal

    def scores_unit(t, s_ref, mask, u):
        h = u // 2
        start = pl.multiple_of(t * blk, blk)
        kb = k_ref[0, pl.ds(start, blk), h * LANES:(h + 1) * LANES]
        sc = lax.dot_general(kb, qs_ref[u], _NT, preferred_element_type=F32)
        s_ref[u] = sc if mask is None else jnp.where(mask, sc, -jnp.inf)

    def scores(t, s_ref, mask):
        for u in units:
            scores_unit(t, s_ref, mask, u)

    def softmax_unit(s_ref, u):
        sh = s_ref[u]
        m = m_ref[u]
        m_new = jnp.maximum(m, jnp.max(sh, axis=0, keepdims=True))
        alpha = jnp.exp2(m - m_new)
        ph = jnp.exp2(sh - m_new)
        l_ref[u] = alpha * l_ref[u] + jnp.sum(ph, axis=0, keepdims=True)
        m_ref[u] = m_new
        return alpha, ph.astype(BF16)

    def accumulate(t, u, alpha, ph):
        h = u // 2
        acc_ref[u] = alpha * acc_ref[u] + _dot(vt_ref[0, t, h * LANES:(h + 1) * LANES, :], ph)

    def absorb(t, s_ref):
        for u in units:
            accumulate(t, u, *softmax_unit(s_ref, u))

    def scores_and_absorb(t_next, next_ref, mask, t, s_ref):
        scores_unit(t_next, next_ref, mask, 0)
        for u in units:
            alpha, ph = softmax_unit(s_ref, u)
            if u + 1 < 2 * n_heads:
                scores_unit(t_next, next_ref, mask, u + 1)
            accumulate(t, u, alpha, ph)

    m_ref[...] = jnp.full_like(m_ref, -jnp.inf)
    l_ref[...] = jnp.zeros_like(l_ref)
    acc_ref[...] = jnp.zeros_like(acc_ref)
    _visit_blocks(qi + 1, scores, absorb, scores_and_absorb, s_a, s_b, first=real_key, last=causal, only=only,
                  unroll=DIFF_STEPS_PER_TRIP)

    lp = lam_ref[...]
    lam = (jnp.exp(jnp.sum(lp[0:1] * lp[1:2], axis=-1, keepdims=True))
           - jnp.exp(jnp.sum(lp[2:3] * lp[3:4], axis=-1, keepdims=True)) + lam_init)
    for h in range(n_heads):
        o = (acc_ref[2 * h] * (1.0 / l_ref[2 * h])
             - lam * (acc_ref[2 * h + 1] * (1.0 / l_ref[2 * h + 1])))
        ms = jnp.mean(o * o, axis=0, keepdims=True)
        o = o * lax.rsqrt(ms + EPS) * (g_ref[...] * (1.0 - lam_init))
        o_ref[0, pl.ds(q_start, blk), h * LANES:(h + 1) * LANES] = o.T.astype(o_ref.dtype)
    return 0


def _diff_attention(q, k, vt, lam_params, subln_g, lam_init, batch, seq_len, n_pad):
    d = DIFF_HEADS * LANES
    width = DIFF_HEADS_PER_STEP * LANES
    groups = d // width
    n_blk = seq_len // ATTN_BLOCK
    q3 = q.reshape(batch, seq_len, d)
    k3 = k.reshape(batch, seq_len, d)
    vt4 = vt.reshape(batch, n_blk, d, ATTN_BLOCK)
    out = pl.pallas_call(
        functools.partial(_diff_attn_kernel, n_heads=DIFF_HEADS_PER_STEP, n_pad=n_pad, lam_init=lam_init,
                          blk=ATTN_BLOCK),
        grid=(batch, groups),
        in_specs=[
            pl.BlockSpec(lam_params.shape, lambda b, g: (0, 0)),
            pl.BlockSpec((1, seq_len, width), lambda b, g: (b, 0, g)),
            pl.BlockSpec((1, seq_len, width), lambda b, g: (b, 0, g)),
            pl.BlockSpec((1, n_blk, width, ATTN_BLOCK), lambda b, g: (b, 0, g, 0)),
            pl.BlockSpec((LANES, 1), lambda b, g: (0, 0)),
        ],
        out_specs=pl.BlockSpec((1, seq_len, width), lambda b, g: (b, 0, g)),
        out_shape=jax.ShapeDtypeStruct((batch, seq_len, d), BF16),
        scratch_shapes=[pltpu.VMEM((2 * DIFF_HEADS_PER_STEP, ATTN_BLOCK, ATTN_BLOCK), F32)] * 2 + [
            pltpu.VMEM((2 * DIFF_HEADS_PER_STEP, 1, ATTN_BLOCK), F32)] * 2 + [
            pltpu.VMEM((2 * DIFF_HEADS_PER_STEP, LANES, ATTN_BLOCK), F32),
            pltpu.VMEM((2 * DIFF_HEADS_PER_STEP, ATTN_BLOCK, LANES), BF16)],
        compiler_params=pltpu.CompilerParams(
            dimension_semantics=("parallel", "parallel"), vmem_limit_bytes=VMEM_LIMIT),
        name="diff_attention",
    )(lam_params, q3, k3, vt4, subln_g.reshape(LANES, 1))
    return out.reshape(batch * seq_len, d)


def _post_attn_kernel(h_ref, o_ref, wo_ref, g_ref, win_ref, wout_ref, out_ref):
    h1 = h_ref[...] + _dot(o_ref[...], wo_ref[...])
    hn = (h1 * _rms_scale(h1) * g_ref[...]).astype(BF16)
    acc = h1
    d_ff = win_ref.shape[1]
    for c0 in range(0, d_ff, FF_CHUNK):
        a = jnp.maximum(_dot(hn, win_ref[:, c0:c0 + FF_CHUNK]), 0.0)
        acc = acc + _dot((a * a).astype(BF16), wout_ref[c0:c0 + FF_CHUNK, :])
    out_ref[...] = acc


def _post_attention(h, o, wo, g, w_in, w_out, *, tm):
    n, d = h.shape
    row = pl.BlockSpec((tm, d), lambda i: (i, 0))
    return pl.pallas_call(
        _post_attn_kernel,
        grid=(n // tm,),
        in_specs=[row, pl.BlockSpec((tm, o.shape[1]), lambda i: (i, 0)), _resident(wo.shape),
                  _resident((1, d)), _resident(w_in.shape), _resident(w_out.shape)],
        out_specs=row,
        out_shape=jax.ShapeDtypeStruct((n, d), F32),
        compiler_params=pltpu.CompilerParams(
            dimension_semantics=("parallel",), vmem_limit_bytes=VMEM_LIMIT),
        name="post_attention",
    )(h, o, wo, g.reshape(1, d), w_in, w_out)


def _rope_tables(seq_len, n_pad):
    half = ROPE_DIM // 2
    inv_freq = jnp.power(ROPE_THETA, -jnp.arange(0, ROPE_DIM, 2, dtype=F32) / ROPE_DIM)
    pos = jnp.maximum(jnp.arange(seq_len) - n_pad, 0).astype(F32)
    ang = pos[:, None] * inv_freq[None, :]
    cos, sin = jnp.cos(ang), jnp.sin(ang)
    seg_lane = jnp.arange(LANES) % HEAD_DIM
    idx = seg_lane % half
    cos_t = jnp.where(seg_lane < ROPE_DIM, cos[:, idx], 1.0)
    s1_t = jnp.where(seg_lane < half, -sin[:, idx], 0.0)
    s2_t = jnp.where((seg_lane >= half) & (seg_lane < ROPE_DIM), sin[:, idx], 0.0)
    return cos_t, s1_t, s2_t


def kernel(x, meta_tokens, a_norm_g, a_w_qkv, a_w_o, kv_norm_g, kv_w, kv_k_norm_g, b_norm_g, b_w_q,
           b_q_norm_g, b_lambda, b_subln_g, b_w_o, mlp_norm_g, mlp_w_in, mlp_w_out):
    batch, s_len, d = x.shape
    n_a = a_w_qkv.shape[0]
    n_b = b_w_q.shape[0]
    real = s_len + N_META
    seq_len = -(-real // ATTN_BLOCK) * ATTN_BLOCK
    n_pad = seq_len - real
    n_rows = batch * seq_len
    tm = ROW_TILE if n_rows % ROW_TILE == 0 else ATTN_BLOCK

    meta = jnp.broadcast_to(meta_tokens.astype(x.dtype)[None], (batch, N_META, d))
    h = jnp.concatenate([jnp.zeros((batch, n_pad, d), x.dtype), meta, x], axis=1)
    h = h.reshape(n_rows, d)
    rope = _rope_tables(seq_len, n_pad)
    k_cols = DIFF_HEADS * 2 * HEAD_DIM
    qk_scale = HEAD_DIM ** -0.5 * math.log2(math.e)

    kv = None
    for layer in range(n_a + n_b):
        if layer < n_a:
            i = layer
            col_scale = jnp.concatenate([jnp.full((d,), qk_scale, F32), jnp.ones((2 * d,), F32)])
            w_qkv = (a_w_qkv[i] * col_scale).astype(BF16)
            qkv = _norm_proj(h, a_norm_g[i], w_qkv, tm=tm)
            o = _sb_attention(qkv, batch, seq_len, n_pad)
            w_o = a_w_o[i]
        else:
            j = layer - n_a
            if kv is None:
                k_gain = jnp.tile(kv_k_norm_g.reshape(2 * HEAD_DIM), DIFF_HEADS)
                kv = _norm_proj(h, kv_norm_g, kv_w[:, :k_cols].astype(BF16), tm=ATTN_BLOCK, seg_gain=k_gain,
                                rope=rope, n_rope_cols=k_cols, seq_len=seq_len,
                                w_t=kv_w[:, k_cols:].T.astype(BF16))
            lam_init = 0.8 - 0.6 * math.exp(-0.3 * layer)
            q_gain = jnp.tile(b_q_norm_g[j].reshape(2 * HEAD_DIM), DIFF_HEADS) * qk_scale
            q = _norm_proj(h, b_norm_g[j], b_w_q[j].astype(BF16), tm=ATTN_BLOCK, seg_gain=q_gain,
                           rope=rope, n_rope_cols=k_cols, seq_len=seq_len)
            o = _diff_attention(q, kv[0], kv[1], b_lambda[j].astype(F32), b_subln_g[j], lam_init, batch,
                                seq_len, n_pad)
            w_o = b_w_o[j]
        h = _post_attention(h, o, w_o.astype(BF16), mlp_norm_g[layer], mlp_w_in[layer].astype(BF16),
                            mlp_w_out[layer].astype(BF16), tm=tm)

    return h.reshape(batch, seq_len, d)[:, n_pad + N_META:]
```

```python
import functools
import math

import jax
import jax.numpy as jnp
from jax import lax
from jax.experimental import pallas as pl
from jax.experimental.pallas import tpu as pltpu

N_META = 16
SB_HEADS = 16
HEAD_DIM = 64
DIFF_HEADS = 8
ROPE_DIM = 16
ROPE_THETA = 500000.0
EPS = 1e-6

LANES = 128
ATTN_BLOCK = 256
ROW_TILE = 512
FF_CHUNK = 1024
SB_PAIRS_PER_STEP = 4
DIFF_HEADS_PER_STEP = 4
SB_STEPS_PER_TRIP = 4
DIFF_STEPS_PER_TRIP = 6
VMEM_LIMIT = 56 * 1024 * 1024

F32 = jnp.float32
BF16 = jnp.bfloat16
_NT = (((1,), (1,)), ((), ()))


def _dot(a, b):
    return jnp.dot(a, b, preferred_element_type=F32)


def _rms_scale(x):
    return lax.rsqrt(jnp.mean(x * x, axis=-1, keepdims=True) + EPS)


def _resident(shape):
    nd = len(shape)
    return pl.BlockSpec(shape, lambda *_: (0,) * nd, pipeline_mode=pl.Buffered(1))


def _norm_proj_kernel(h_ref, g_ref, w_ref, *rest, n_rope_cols, transposed_extra):
    wt_ref = ot_ref = None
    if transposed_extra:
        wt_ref, *rest, ot_ref = rest
    if n_rope_cols:
        sg_ref, cos_ref, s1_ref, s2_ref, o_ref = rest
    else:
        (o_ref,) = rest
    x = h_ref[...]
    hn = (x * _rms_scale(x) * g_ref[...]).astype(BF16)
    n_out = o_ref.shape[1]
    if n_rope_cols:
        r = lax.broadcasted_iota(jnp.int32, (LANES, LANES), 0) // HEAD_DIM
        c = lax.broadcasted_iota(jnp.int32, (LANES, LANES), 1) // HEAD_DIM
        seg = jnp.where(r == c, 1.0, 0.0).astype(BF16)
        cos, s1, s2 = cos_ref[...], s1_ref[...], s2_ref[...]
    chunk = min(FF_CHUNK, n_out)
    for c0 in range(0, n_out, chunk):
        y = _dot(hn, w_ref[:, c0:c0 + chunk])
        if c0 >= n_rope_cols:
            o_ref[:, c0:c0 + chunk] = y.astype(o_ref.dtype)
            continue
        for l0 in range(0, chunk, LANES):
            yc = y[:, l0:l0 + LANES]
            ms = _dot((yc * yc).astype(BF16), seg) * (1.0 / HEAD_DIM)
            yn = yc * lax.rsqrt(ms + EPS) * sg_ref[:, c0 + l0:c0 + l0 + LANES]
            rot = (yn * cos
                   + pltpu.roll(yn, LANES - ROPE_DIM // 2, 1) * s1
                   + pltpu.roll(yn, ROPE_DIM // 2, 1) * s2)
            o_ref[:, c0 + l0:c0 + l0 + LANES] = rot.astype(o_ref.dtype)
    if transposed_extra:
        ot_ref[0] = lax.dot_general(wt_ref[...], hn, _NT, preferred_element_type=F32).astype(ot_ref.dtype)


def _norm_proj(h, g, w, *, tm, seg_gain=None, rope=None, n_rope_cols=0, seq_len=None, w_t=None):
    n, d = h.shape
    n_out = w.shape[1]
    in_specs = [pl.BlockSpec((tm, d), lambda i: (i, 0)), _resident((1, d)), _resident((d, n_out))]
    args = [h, g.reshape(1, d), w]
    if w_t is not None:
        in_specs.append(_resident(w_t.shape))
        args.append(w_t)
    if n_rope_cols:
        tiles_per_seq = seq_len // tm
        tab = pl.BlockSpec((tm, LANES), lambda i: (i % tiles_per_seq, 0))
        in_specs += [_resident((1, n_rope_cols)), tab, tab, tab]
        args += [seg_gain.reshape(1, n_rope_cols), *rope]
    out_specs = pl.BlockSpec((tm, n_out), lambda i: (i, 0))
    out_shape = jax.ShapeDtypeStruct((n, n_out), BF16)
    if w_t is not None:
        out_specs = [out_specs, pl.BlockSpec((1, w_t.shape[0], tm), lambda i: (i, 0, 0))]
        out_shape = [out_shape, jax.ShapeDtypeStruct((n // tm, w_t.shape[0], tm), BF16)]
    return pl.pallas_call(
        functools.partial(_norm_proj_kernel, n_rope_cols=n_rope_cols, transposed_extra=w_t is not None),
        grid=(n // tm,),
        in_specs=in_specs,
        out_specs=out_specs,
        out_shape=out_shape,
        compiler_params=pltpu.CompilerParams(
            dimension_semantics=("parallel",), vmem_limit_bytes=VMEM_LIMIT),
        name="norm_proj_rope" if n_rope_cols else "norm_proj",
    )(*args)


def _stack_halves(q, lo_half):
    zero = jnp.zeros_like(q)
    return jnp.concatenate([jnp.where(lo_half, q, zero), jnp.where(lo_half, zero, q)], axis=0)


def _neg_abs(x):
    sign = jnp.uint32(0x80000000)
    return lax.bitcast_convert_type(lax.bitcast_convert_type(x, jnp.uint32) | sign, F32)


def _all_of(*masks):
    masks = [m for m in masks if m is not None]
    return functools.reduce(jnp.logical_and, masks) if masks else None


def _visit_blocks(n_steps, produce, consume, produce_and_consume, buf_a, buf_b, first, last, only, unroll):
    assert unroll % 2 == 0
    bufs = (buf_a, buf_b)

    @pl.when(n_steps == 1)
    def _():
        produce(0, buf_a, only)
        consume(0, buf_a)

    @pl.when(n_steps > 1)
    def _():
        produce(0, buf_a, first)

        def trip(k, _):
            t = unroll * k
            for i in range(unroll):
                produce_and_consume(t + i + 1, bufs[(i + 1) % 2], None, t + i, bufs[i % 2])
            return 0

        n_trips = (n_steps - 2) // unroll
        lax.fori_loop(0, n_trips, trip, 0)
        base = unroll * n_trips
        left = n_steps - 1 - base
        for m in range(1, unroll + 1):
            @pl.when(left == m)
            def _(m=m):
                for i in range(1, m + 1):
                    produce_and_consume(base + i, bufs[i % 2], last if i == m else None,
                                        base + i - 1, bufs[(i - 1) % 2])
                consume(base + m, bufs[m % 2])


def _sb_attn_kernel(q_ref, k_ref, v_ref, o_ref, y_a, y_b, carry_ref, acc_ref, qs_ref, *, n_pairs, n_pad, blk):
    lax.fori_loop(0, q_ref.shape[1] // blk, functools.partial(
        _sb_query_block, q_ref, k_ref, v_ref, o_ref, y_a, y_b, carry_ref, acc_ref, qs_ref, n_pairs, n_pad, blk), 0)


def _sb_query_block(q_ref, k_ref, v_ref, o_ref, y_a, y_b, carry_ref, acc_ref, qs_ref, n_pairs, n_pad, blk, qi, _):
    pairs = range(n_pairs)
    q_start = pl.multiple_of(qi * blk, blk)
    lo_half = lax.broadcasted_iota(jnp.int32, (1, LANES), 1) < HEAD_DIM
    for p in pairs:
        qs_ref[p] = _stack_halves(q_ref[0, pl.ds(q_start, blk), p * LANES:(p + 1) * LANES], lo_half)
    qs = [qs_ref.at[p] for p in pairs]
    row = lax.broadcasted_iota(jnp.int32, (2 * blk, blk), 0)
    col = lax.broadcasted_iota(jnp.int32, (2 * blk, blk), 1)
    key_j = lax.broadcasted_iota(jnp.int32, (blk, blk), 0)
    key_s = lax.broadcasted_iota(jnp.int32, (blk, blk), 1)
    neg_tri = jnp.where(key_j > key_s, -1.0, 0.0).astype(BF16)
    past = col < row % blk
    real_key = col >= n_pad if n_pad else None

    def logits_pair(t, y_ref, mask, p):
        start = pl.multiple_of((qi - t) * blk, blk)
        kb = k_ref[0, pl.ds(start, blk), p * LANES:(p + 1) * LANES]
        y = lax.dot_general(qs[p][...], kb, _NT, preferred_element_type=F32)
        y_ref[p] = y if mask is None else jnp.where(mask, y, -jnp.inf)

    def logits(t, y_ref, mask):
        for p in pairs:
            logits_pair(t, y_ref, mask, p)

    def cum_pair(y_ref, p):
        y = y_ref[p]
        sp = jnp.maximum(y, 0.0) + jnp.log2(1.0 + jnp.exp2(_neg_abs(y)))
        return _dot(sp.astype(BF16), neg_tri), sp[:, 0:1], y - sp

    def weigh_pair(t, y_ref, p, cum_first_own):
        cum, sp_first, own = cum_first_own
        start = pl.multiple_of((qi - t) * blk, blk)
        vb = v_ref[0, pl.ds(start, blk), p * LANES:(p + 1) * LANES]
        carry = carry_ref[p]
        w = jnp.exp2((own + cum + jnp.concatenate([carry, carry], axis=1)).astype(BF16))
        acc_ref[p, 0] += _dot(w[:blk], vb)
        acc_ref[p, 1] += _dot(w[blk:], vb)
        carry_ref[p] = carry + jnp.broadcast_to(cum[:, 0:1] - sp_first, carry.shape)

    def absorb(t, y_ref):
        cum = [cum_pair(y_ref, p) for p in pairs]
        for p in pairs:
            weigh_pair(t, y_ref, p, cum[p])

    def logits_and_absorb(t_next, next_ref, mask, t, y_ref):
        cum = {}
        for p in pairs:
            logits_pair(t_next, next_ref, mask, p)
            cum[p] = cum_pair(y_ref, p)
            if p > 0:
                weigh_pair(t, y_ref, p - 1, cum[p - 1])
        weigh_pair(t, y_ref, n_pairs - 1, cum[n_pairs - 1])

    carry_ref[...] = jnp.zeros_like(carry_ref)
    acc_ref[...] = jnp.zeros_like(acc_ref)
    _visit_blocks(qi + 1, logits, absorb, logits_and_absorb, y_a, y_b,
                  first=past, last=real_key, only=_all_of(past, real_key), unroll=SB_STEPS_PER_TRIP)
    for p in pairs:
        o_ref[0, pl.ds(q_start, blk), p * LANES:(p + 1) * LANES] = jnp.where(
            lo_half, acc_ref[p, 0], acc_ref[p, 1]).astype(o_ref.dtype)
    return 0


def _sb_attention(qkv, batch, seq_len, n_pad):
    d = SB_HEADS * HEAD_DIM
    width = SB_PAIRS_PER_STEP * LANES
    groups = d // width
    qkv3 = qkv.reshape(batch, seq_len, 3 * d)
    out = pl.pallas_call(
        functools.partial(_sb_attn_kernel, n_pairs=SB_PAIRS_PER_STEP, n_pad=n_pad, blk=ATTN_BLOCK),
        grid=(batch, groups),
        in_specs=[
            pl.BlockSpec((1, seq_len, width), lambda b, g: (b, 0, g)),
            pl.BlockSpec((1, seq_len, width), lambda b, g: (b, 0, groups + g)),
            pl.BlockSpec((1, seq_len, width), lambda b, g: (b, 0, 2 * groups + g)),
        ],
        out_specs=pl.BlockSpec((1, seq_len, width), lambda b, g: (b, 0, g)),
        out_shape=jax.ShapeDtypeStruct((batch, seq_len, d), BF16),
        scratch_shapes=[pltpu.VMEM((SB_PAIRS_PER_STEP, 2 * ATTN_BLOCK, ATTN_BLOCK), F32)] * 2 + [
            pltpu.VMEM((SB_PAIRS_PER_STEP, 2 * ATTN_BLOCK, LANES), F32),
            pltpu.VMEM((SB_PAIRS_PER_STEP, 2, ATTN_BLOCK, LANES), F32),
            pltpu.VMEM((SB_PAIRS_PER_STEP, 2 * ATTN_BLOCK, LANES), BF16)],
        compiler_params=pltpu.CompilerParams(
            dimension_semantics=("parallel", "parallel"), vmem_limit_bytes=VMEM_LIMIT),
        name="sb_attention",
    )(qkv3, qkv3, qkv3)
    return out.reshape(batch * seq_len, d)


def _diff_attn_kernel(lam_ref, q_ref, k_ref, vt_ref, g_ref, o_ref, s_a, s_b, m_ref, l_ref, acc_ref, qs_ref, *,
                      n_heads, n_pad, lam_init, blk):
    lax.fori_loop(0, q_ref.shape[1] // blk, functools.partial(
        _diff_query_block, lam_ref, q_ref, k_ref, vt_ref, g_ref, o_ref, s_a, s_b, m_ref, l_ref, acc_ref, qs_ref,
        n_heads, n_pad, lam_init, blk), 0)


def _diff_query_block(lam_ref, q_ref, k_ref, vt_ref, g_ref, o_ref, s_a, s_b, m_ref, l_ref, acc_ref, qs_ref,
                      n_heads, n_pad, lam_init, blk, qi, _):
    units = range(2 * n_heads)
    q_start = pl.multiple_of(qi * blk, blk)
    lo_half = lax.broadcasted_iota(jnp.int32, (1, LANES), 1) < HEAD_DIM
    for h in range(n_heads):
        q = q_ref[0, pl.ds(q_start, blk), h * LANES:(h + 1) * LANES]
        zero = jnp.zeros_like(q)
        qs_ref[2 * h] = jnp.where(lo_half, q, zero)
        qs_ref[2 * h + 1] = jnp.where(lo_half, zero, q)
    key = lax.broadcasted_iota(jnp.int32, (blk, blk), 0)
    qry = lax.broadcasted_iota(jnp.int32, (blk, blk), 1)
    causal = key <= qry
    real_key = key >= n_pad if n_pad else None
    only = _all_of(causal, jnp.logical_or(real_key, qry < n_pad)) if n_pad else causal

    def scores_unit(t, s_ref, mask, u):
        h = u // 2
        start = pl.multiple_of(t * blk, blk)
        kb = k_ref[0, pl.ds(start, blk), h * LANES:(h + 1) * LANES]
        sc = lax.dot_general(kb, qs_ref[u], _NT, preferred_element_type=F32)
        s_ref[u] = sc if mask is None else jnp.where(mask, sc, -jnp.inf)

    def scores(t, s_ref, mask):
        for u in units:
            scores_unit(t, s_ref, mask, u)

    def softmax_unit(s_ref, u):
        sh = s_ref[u]
        m = m_ref[u]
        m_new = jnp.maximum(m, jnp.max(sh, axis=0, keepdims=True))
        alpha = jnp.exp2(m - m_new)
        ph = jnp.exp2(sh - m_new)
        l_ref[u] = alpha * l_ref[u] + jnp.sum(ph, axis=0, keepdims=True)
        m_ref[u] = m_new
        return alpha, ph.astype(BF16)

    def accumulate(t, u, alpha, ph):
        h = u // 2
        acc_ref[u] = alpha * acc_ref[u] + _dot(vt_ref[0, t, h * LANES:(h + 1) * LANES, :], ph)

    def absorb(t, s_ref):
        for u in units:
            accumulate(t, u, *softmax_unit(s_ref, u))

    def scores_and_absorb(t_next, next_ref, mask, t, s_ref):
        scores_unit(t_next, next_ref, mask, 0)
        for u in units:
            alpha, ph = softmax_unit(s_ref, u)
            if u + 1 < 2 * n_heads:
                scores_unit(t_next, next_ref, mask, u + 1)
            accumulate(t, u, alpha, ph)

    m_ref[...] = jnp.full_like(m_ref, -jnp.inf)
    l_ref[...] = jnp.zeros_like(l_ref)
    acc_ref[...] = jnp.zeros_like(acc_ref)
    _visit_blocks(qi + 1, scores, absorb, scores_and_absorb, s_a, s_b, first=real_key, last=causal, only=only,
                  unroll=DIFF_STEPS_PER_TRIP)

    lp = lam_ref[...]
    lam = (jnp.exp(jnp.sum(lp[0:1] * lp[1:2], axis=-1, keepdims=True))
           - jnp.exp(jnp.sum(lp[2:3] * lp[3:4], axis=-1, keepdims=True)) + lam_init)
    for h in range(n_heads):
        o = (acc_ref[2 * h] * (1.0 / l_ref[2 * h])
             - lam * (acc_ref[2 * h + 1] * (1.0 / l_ref[2 * h + 1])))
        ms = jnp.mean(o * o, axis=0, keepdims=True)
        o = o * lax.rsqrt(ms + EPS) * (g_ref[...] * (1.0 - lam_init))
        o_ref[0, pl.ds(q_start, blk), h * LANES:(h + 1) * LANES] = o.T.astype(o_ref.dtype)
    return 0


def _diff_attention(q, k, vt, lam_params, subln_g, lam_init, batch, seq_len, n_pad):
    d = DIFF_HEADS * LANES
    width = DIFF_HEADS_PER_STEP * LANES
    groups = d // width
    n_blk = seq_len // ATTN_BLOCK
    q3 = q.reshape(batch, seq_len, d)
    k3 = k.reshape(batch, seq_len, d)
    vt4 = vt.reshape(batch, n_blk, d, ATTN_BLOCK)
    out = pl.pallas_call(
        functools.partial(_diff_attn_kernel, n_heads=DIFF_HEADS_PER_STEP, n_pad=n_pad, lam_init=lam_init,
                          blk=ATTN_BLOCK),
        grid=(batch, groups),
        in_specs=[
            pl.BlockSpec(lam_params.shape, lambda b, g: (0, 0)),
            pl.BlockSpec((1, seq_len, width), lambda b, g: (b, 0, g)),
            pl.BlockSpec((1, seq_len, width), lambda b, g: (b, 0, g)),
            pl.BlockSpec((1, n_blk, width, ATTN_BLOCK), lambda b, g: (b, 0, g, 0)),
            pl.BlockSpec((LANES, 1), lambda b, g: (0, 0)),
        ],
        out_specs=pl.BlockSpec((1, seq_len, width), lambda b, g: (b, 0, g)),
        out_shape=jax.ShapeDtypeStruct((batch, seq_len, d), BF16),
        scratch_shapes=[pltpu.VMEM((2 * DIFF_HEADS_PER_STEP, ATTN_BLOCK, ATTN_BLOCK), F32)] * 2 + [
            pltpu.VMEM((2 * DIFF_HEADS_PER_STEP, 1, ATTN_BLOCK), F32)] * 2 + [
            pltpu.VMEM((2 * DIFF_HEADS_PER_STEP, LANES, ATTN_BLOCK), F32),
            pltpu.VMEM((2 * DIFF_HEADS_PER_STEP, ATTN_BLOCK, LANES), BF16)],
        compiler_params=pltpu.CompilerParams(
            dimension_semantics=("parallel", "parallel"), vmem_limit_bytes=VMEM_LIMIT),
        name="diff_attention",
    )(lam_params, q3, k3, vt4, subln_g.reshape(LANES, 1))
    return out.reshape(batch * seq_len, d)


def _post_attn_kernel(h_ref, o_ref, wo_ref, g_ref, win_ref, wout_ref, out_ref):
    h1 = h_ref[...] + _dot(o_ref[...], wo_ref[...])
    hn = (h1 * _rms_scale(h1) * g_ref[...]).astype(BF16)
    acc = h1
    d_ff = win_ref.shape[1]
    for c0 in range(0, d_ff, FF_CHUNK):
        a = jnp.maximum(_dot(hn, win_ref[:, c0:c0 + FF_CHUNK]), 0.0)
        acc = acc + _dot((a * a).astype(BF16), wout_ref[c0:c0 + FF_CHUNK, :])
    out_ref[...] = acc


def _post_attention(h, o, wo, g, w_in, w_out, *, tm):
    n, d = h.shape
    row = pl.BlockSpec((tm, d), lambda i: (i, 0))
    return pl.pallas_call(
        _post_attn_kernel,
        grid=(n // tm,),
        in_specs=[row, pl.BlockSpec((tm, o.shape[1]), lambda i: (i, 0)), _resident(wo.shape),
                  _resident((1, d)), _resident(w_in.shape), _resident(w_out.shape)],
        out_specs=row,
        out_shape=jax.ShapeDtypeStruct((n, d), F32),
        compiler_params=pltpu.CompilerParams(
            dimension_semantics=("parallel",), vmem_limit_bytes=VMEM_LIMIT),
        name="post_attention",
    )(h, o, wo, g.reshape(1, d), w_in, w_out)


def _rope_tables(seq_len, n_pad):
    half = ROPE_DIM // 2
    inv_freq = jnp.power(ROPE_THETA, -jnp.arange(0, ROPE_DIM, 2, dtype=F32) / ROPE_DIM)
    pos = jnp.maximum(jnp.arange(seq_len) - n_pad, 0).astype(F32)
    ang = pos[:, None] * inv_freq[None, :]
    cos, sin = jnp.cos(ang), jnp.sin(ang)
    seg_lane = jnp.arange(LANES) % HEAD_DIM
    idx = seg_lane % half
    cos_t = jnp.where(seg_lane < ROPE_DIM, cos[:, idx], 1.0)
    s1_t = jnp.where(seg_lane < half, -sin[:, idx], 0.0)
    s2_t = jnp.where((seg_lane >= half) & (seg_lane < ROPE_DIM), sin[:, idx], 0.0)
    return cos_t, s1_t, s2_t


def kernel(x, meta_tokens, a_norm_g, a_w_qkv, a_w_o, kv_norm_g, kv_w, kv_k_norm_g, b_norm_g, b_w_q,
           b_q_norm_g, b_lambda, b_subln_g, b_w_o, mlp_norm_g, mlp_w_in, mlp_w_out):
    batch, s_len, d = x.shape
    n_a = a_w_qkv.shape[0]
    n_b = b_w_q.shape[0]
    real = s_len + N_META
    seq_len = -(-real // ATTN_BLOCK) * ATTN_BLOCK
    n_pad = seq_len - real
    n_rows = batch * seq_len
    tm = ROW_TILE if n_rows % ROW_TILE == 0 else ATTN_BLOCK

    meta = jnp.broadcast_to(meta_tokens.astype(x.dtype)[None], (batch, N_META, d))
    h = jnp.concatenate([jnp.zeros((batch, n_pad, d), x.dtype), meta, x], axis=1)
    h = h.reshape(n_rows, d)
    rope = _rope_tables(seq_len, n_pad)
    k_cols = DIFF_HEADS * 2 * HEAD_DIM
    qk_scale = HEAD_DIM ** -0.5 * math.log2(math.e)

    kv = None
    for layer in range(n_a + n_b):
        if layer < n_a:
            i = layer
            col_scale = jnp.concatenate([jnp.full((d,), qk_scale, F32), jnp.ones((2 * d,), F32)])
            w_qkv = (a_w_qkv[i] * col_scale).astype(BF16)
            qkv = _norm_proj(h, a_norm_g[i], w_qkv, tm=tm)
            o = _sb_attention(qkv, batch, seq_len, n_pad)
            w_o = a_w_o[i]
        else:
            j = layer - n_a
            if kv is None:
                k_gain = jnp.tile(kv_k_norm_g.reshape(2 * HEAD_DIM), DIFF_HEADS)
                kv = _norm_proj(h, kv_norm_g, kv_w[:, :k_cols].astype(BF16), tm=ATTN_BLOCK, seg_gain=k_gain,
                                rope=rope, n_rope_cols=k_cols, seq_len=seq_len,
                                w_t=kv_w[:, k_cols:].T.astype(BF16))
            lam_init = 0.8 - 0.6 * math.exp(-0.3 * layer)
            q_gain = jnp.tile(b_q_norm_g[j].reshape(2 * HEAD_DIM), DIFF_HEADS) * qk_scale
            q = _norm_proj(h, b_norm_g[j], b_w_q[j].astype(BF16), tm=ATTN_BLOCK, seg_gain=q_gain,
                           rope=rope, n_rope_cols=k_cols, seq_len=seq_len)
            o = _diff_attention(q, kv[0], kv[1], b_lambda[j].astype(F32), b_subln_g[j], lam_init, batch,
                                seq_len, n_pad)
            w_o = b_w_o[j]
        h = _post_attention(h, o, w_o.astype(BF16), mlp_norm_g[layer], mlp_w_in[layer].astype(BF16),
                            mlp_w_out[layer].astype(BF16), tm=tm)

    return h.reshape(batch, seq_len, d)[:, n_pad + N_META:]
```
